```python
import jax
import jax.numpy as jnp
from jax import lax
import numpy as np

D_MODEL = 1024
BATCH = 8
SEQ = 2048
DEPTH = 2

N_MIXERS = 2
N_HEADS = 16
HEAD_DIM = D_MODEL // N_HEADS
KV_LATENT = D_MODEL // 4
IDX_HEADS = 8
IDX_DIM = 64
TOPK_MAX = 256
Q_BLOCK = 128
POOL_WINDOWS = (2, 4, 8, 16)
POOL_GROUPS = len(POOL_WINDOWS)
POOL_GROUP_DIM = D_MODEL // POOL_GROUPS
D_FF = -(-(8 * D_MODEL) // (3 * 256)) * 256
Q_COLS = N_HEADS * HEAD_DIM
QIDX_COLS = IDX_HEADS * IDX_DIM
A_IN_DIM = Q_COLS + KV_LATENT + QIDX_COLS + IDX_DIM + IDX_HEADS
N_A_LAYERS = (DEPTH + N_MIXERS - 1) // N_MIXERS
N_B_LAYERS = DEPTH // N_MIXERS
DEEPNORM_ALPHA = (2 * DEPTH) ** 0.25
DEEPNORM_BETA = (8 * DEPTH) ** -0.25
LN_EPS = 1e-5
RMS_EPS = 1e-6

kernel_name = 'hybrid_dsa_pool_deepnorm'


def alibi_slopes(n_heads):
    return jnp.exp2(-8.0 * jnp.arange(1, n_heads + 1, dtype=jnp.float32) / n_heads)


def layer_norm(x, g, b):
    xf = x.astype(jnp.float32)
    mu = jnp.mean(xf, axis=-1, keepdims=True)
    var = jnp.mean(jnp.square(xf - mu), axis=-1, keepdims=True)
    y = (xf - mu) * lax.rsqrt(var + LN_EPS)
    return (y * g.astype(jnp.float32) + b.astype(jnp.float32)).astype(x.dtype)


def rms_norm(x, g):
    xf = x.astype(jnp.float32)
    y = xf * lax.rsqrt(jnp.mean(jnp.square(xf), axis=-1, keepdims=True) + RMS_EPS)
    return (y * g.astype(jnp.float32)).astype(x.dtype)


def to_blocks(t, n_blocks):
    t = t.reshape((t.shape[0], n_blocks, Q_BLOCK) + t.shape[2:])
    return jnp.swapaxes(t, 0, 1)


def sparse_indexer_attention(h, w_in, w_uk, w_uv, kv_norm_g, w_o):
    bsz, seq, _ = h.shape
    topk = min(TOPK_MAX, seq // 4)
    n_blocks = seq // Q_BLOCK
    proj = h @ w_in
    offs = [Q_COLS, Q_COLS + KV_LATENT, Q_COLS + KV_LATENT + QIDX_COLS,
            Q_COLS + KV_LATENT + QIDX_COLS + IDX_DIM]
    q, c_kv, q_idx, k_idx, w_idx = jnp.split(proj, offs, axis=-1)
    q = q.reshape(bsz, seq, N_HEADS, HEAD_DIM)
    c_kv = rms_norm(c_kv, kv_norm_g)
    q_lat = jnp.einsum('blhd,hcd->blhc', q, w_uk) * (HEAD_DIM ** -0.5)
    q_idx = q_idx.reshape(bsz, seq, IDX_HEADS, IDX_DIM) * (IDX_DIM ** -0.5)
    w_idx = w_idx * (IDX_HEADS ** -0.5)
    slopes = alibi_slopes(N_HEADS)
    key_pos = jnp.arange(seq, dtype=jnp.int32)
    q_pos = key_pos.reshape(n_blocks, Q_BLOCK)

    def block(args):
        ql, qi, wi, qp = args
        rel = jax.nn.relu(jnp.einsum('bqhd,bsd->bqhs', qi, k_idx))
        score = jnp.einsum('bqh,bqhs->bqs', wi, rel).astype(jnp.float32)
        causal = key_pos[None, :] <= qp[:, None]
        score = jnp.where(causal[None], score, -jnp.inf)
        _, sel = lax.top_k(score, topk)
        valid = sel <= qp[None, :, None]
        c_sel = jax.vmap(lambda cb, ib: cb[ib])(c_kv, sel)
        logits = jnp.einsum('bqhc,bqkc->bhqk', ql, c_sel).astype(jnp.float32)
        dist = (qp[None, :, None] - sel).astype(jnp.float32)
        logits = logits - slopes[None, :, None, None] * dist[:, None]
        logits = jnp.where(valid[:, None], logits, -jnp.inf)
        p = jax.nn.softmax(logits, axis=-1).astype(c_sel.dtype)
        return jnp.einsum('bhqk,bqkc->bqhc', p, c_sel)

    o_lat = lax.map(block, (to_blocks(q_lat, n_blocks), to_blocks(q_idx, n_blocks),
                            to_blocks(w_idx, n_blocks), q_pos))
    o_lat = jnp.swapaxes(o_lat, 0, 1).reshape(bsz, seq, N_HEADS, KV_LATENT)
    o = jnp.einsum('blhc,hcd->blhd', o_lat, w_uv).reshape(bsz, seq, Q_COLS)
    return o @ w_o


def multiscale_pool_mixer(h, w_in, w_grp, scale, w_o):
    bsz, seq, _ = h.shape
    u = (h @ w_in).reshape(bsz, seq, POOL_GROUPS, POOL_GROUP_DIM)
    uf = u.astype(jnp.float32)
    cs = jnp.concatenate([jnp.zeros_like(uf[:, :1]), jnp.cumsum(uf, axis=1)], axis=1)
    end = jnp.arange(1, seq + 1, dtype=jnp.int32)[:, None]
    win = jnp.array(POOL_WINDOWS, dtype=jnp.int32)[None, :]
    start = jnp.maximum(end - win, 0)
    count = (end - start).astype(jnp.float32)
    g_idx = jnp.arange(POOL_GROUPS, dtype=jnp.int32)[None, :]
    mean = (cs[:, 1:] - cs[:, start, g_idx]) / count[None, :, :, None]
    pooled = (mean - uf).astype(h.dtype)
    y = jnp.einsum('blgc,gcd->blgd', pooled, w_grp).reshape(bsz, seq, D_MODEL) * scale
    return y @ w_o


def swiglu_ffn(h, w_gu, w_down):
    gate, up = jnp.split(h @ w_gu, 2, axis=-1)
    return (jax.nn.silu(gate) * up) @ w_down


def setup_inputs(seed: int = 0) -> dict:
    key = jax.random.key(seed)
    ks = jax.random.split(key, 17)
    nrm = jax.random.normal
    f32 = jnp.float32
    d = D_MODEL
    return {
        'x': nrm(ks[0], (BATCH, SEQ, d), f32),
        'a_w_in': nrm(ks[1], (N_A_LAYERS, d, A_IN_DIM), f32) * d ** -0.5,
        'a_w_uk': nrm(ks[2], (N_A_LAYERS, N_HEADS, KV_LATENT, HEAD_DIM), f32) * KV_LATENT ** -0.5,
        'a_w_uv': nrm(ks[3], (N_A_LAYERS, N_HEADS, KV_LATENT, HEAD_DIM), f32) * KV_LATENT ** -0.5,
        'a_kv_norm_g': 1.0 + 0.02 * nrm(ks[4], (N_A_LAYERS, KV_LATENT), f32),
        'a_w_o': nrm(ks[5], (N_A_LAYERS, Q_COLS, d), f32) * (Q_COLS ** -0.5) * DEEPNORM_BETA,
        'b_w_in': nrm(ks[6], (N_B_LAYERS, d, d), f32) * d ** -0.5,
        'b_w_grp': nrm(ks[7], (N_B_LAYERS, POOL_GROUPS, POOL_GROUP_DIM, POOL_GROUP_DIM), f32) * POOL_GROUP_DIM ** -0.5,
        'b_scale': 1.0 + 0.1 * nrm(ks[8], (N_B_LAYERS, d), f32),
        'b_w_o': nrm(ks[9], (N_B_LAYERS, d, d), f32) * (d ** -0.5) * DEEPNORM_BETA,
        'f_w_gu': nrm(ks[10], (DEPTH, d, 2 * D_FF), f32) * d ** -0.5,
        'f_w_down': nrm(ks[11], (DEPTH, D_FF, d), f32) * (D_FF ** -0.5) * DEEPNORM_BETA,
        'ln_mix_g': 1.0 + 0.02 * nrm(ks[12], (DEPTH, d), f32),
        'ln_mix_b': 0.02 * nrm(ks[13], (DEPTH, d), f32),
        'ln_ffn_g': 1.0 + 0.02 * nrm(ks[14], (DEPTH, d), f32),
        'ln_ffn_b': 0.02 * nrm(ks[15], (DEPTH, d), f32),
    }


def reference(x, a_w_in, a_w_uk, a_w_uv, a_kv_norm_g, a_w_o, b_w_in, b_w_grp, b_scale, b_w_o,
              f_w_gu, f_w_down, ln_mix_g, ln_mix_b, ln_ffn_g, ln_ffn_b):
    h = x
    for i in range(DEPTH):
        j = i // N_MIXERS
        if i % N_MIXERS == 0:
            mix = sparse_indexer_attention(h, a_w_in[j], a_w_uk[j], a_w_uv[j], a_kv_norm_g[j], a_w_o[j])
        else:
            mix = multiscale_pool_mixer(h, b_w_in[j], b_w_grp[j], b_scale[j], b_w_o[j])
        h = layer_norm(DEEPNORM_ALPHA * h + mix, ln_mix_g[i], ln_mix_b[i])
        h = layer_norm(DEEPNORM_ALPHA * h + swiglu_ffn(h, f_w_gu[i], f_w_down[i]), ln_ffn_g[i], ln_ffn_b[i])
    return h
```

```python
import functools

import jax
import jax.numpy as jnp
from jax import lax
from jax.experimental import pallas as pl
from jax.experimental.pallas import tpu as pltpu

D_MODEL = 1024
N_HEADS = 16
HEAD_DIM = 64
KV_LATENT = 256
IDX_HEADS = 8
IDX_DIM = 64
TOPK = 256
POOL_WINDOWS = (2, 4, 8, 16)
POOL_GROUP_DIM = 256
D_FF = 2816
DEPTH = 2
DEEPNORM_ALPHA = (2 * DEPTH) ** 0.25
LN_EPS = 1e-5
RMS_EPS = 1e-6

Q_TILE = 128
K_CHUNK = 256
HEAD_GROUP = 4
N_HEAD_GROUPS = N_HEADS // HEAD_GROUP
FF_CHUNK = 256
N_FF_CHUNKS = D_FF // FF_CHUNK
POOL_HALO = 16
MASK_BIAS = -1e30
MAX_SEARCH_ITERS = 4096

BF16 = jnp.bfloat16
F32 = jnp.float32
VMEM_LIMIT = 56 * 1024 * 1024


def _nt_dot(a, b):
    return lax.dot_general(a, b, (((1,), (1,)), ((), ())), preferred_element_type=F32)


def _layer_norm(z, g, b):
    mu = jnp.mean(z, axis=-1, keepdims=True)
    zc = z - mu
    var = jnp.mean(zc * zc, axis=-1, keepdims=True)
    return zc * lax.rsqrt(var + LN_EPS) * g + b


def _proj_kernel(x_ref, w1_ref, wqit_ref, wwit_ref, wukbd_ref, g_ref,
                 qlat_ref, c_ref, ki_ref, qit_ref, wit_ref):
    xb = x_ref[...].astype(BF16)
    main = jnp.dot(xb, w1_ref[...], preferred_element_type=F32)
    ckv = main[:, D_MODEL:D_MODEL + KV_LATENT]
    ms = jnp.mean(ckv * ckv, axis=-1, keepdims=True)
    c_ref[0] = (ckv * lax.rsqrt(ms + RMS_EPS) * g_ref[...]).astype(BF16)
    ki_ref[0] = main[:, D_MODEL + KV_LATENT:].astype(BF16)
    for g4 in range(N_HEAD_GROUPS):
        qg = main[:, g4 * 256:(g4 + 1) * 256].astype(BF16)
        ql = jnp.dot(qg, wukbd_ref[g4], preferred_element_type=F32) * (HEAD_DIM ** -0.5)
        for hh in range(HEAD_GROUP):
            qlat_ref[0, g4 * HEAD_GROUP + hh] = ql[:, hh * KV_LATENT:(hh + 1) * KV_LATENT].astype(BF16)
    qit_ref[...] = (_nt_dot(wqit_ref[...], xb) * (IDX_DIM ** -0.5)).astype(BF16)
    wit_ref[...] = _nt_dot(wwit_ref[...], xb) * (IDX_HEADS ** -0.5)


def _proj_call(x2, w1, wqit, wwit, wukbd, g, bsz, seq, tm):
    t = bsz * seq
    per = seq // tm
    return pl.pallas_call(
        _proj_kernel,
        grid=(t // tm,),
        in_specs=[
            pl.BlockSpec((tm, D_MODEL), lambda i: (i, 0)),
            pl.BlockSpec(w1.shape, lambda i: (0, 0)),
            pl.BlockSpec(wqit.shape, lambda i: (0, 0)),
            pl.BlockSpec(wwit.shape, lambda i: (0, 0)),
            pl.BlockSpec(wukbd.shape, lambda i: (0, 0, 0)),
            pl.BlockSpec(g.shape, lambda i: (0, 0)),
        ],
        out_specs=[
            pl.BlockSpec((1, N_HEADS, tm, KV_LATENT), lambda i: (i // per, 0, i % per, 0)),
            pl.BlockSpec((1, tm, KV_LATENT), lambda i: (i // per, i % per, 0)),
            pl.BlockSpec((1, tm, IDX_DIM), lambda i: (i // per, i % per, 0)),
            pl.BlockSpec((IDX_HEADS * IDX_DIM, tm), lambda i: (0, i)),
            pl.BlockSpec((IDX_HEADS, tm), lambda i: (0, i)),
        ],
        out_shape=[
            jax.ShapeDtypeStruct((bsz, N_HEADS, seq, KV_LATENT), BF16),
            jax.ShapeDtypeStruct((bsz, seq, KV_LATENT), BF16),
            jax.ShapeDtypeStruct((bsz, seq, IDX_DIM), BF16),
            jax.ShapeDtypeStruct((IDX_HEADS * IDX_DIM, t), BF16),
            jax.ShapeDtypeStruct((IDX_HEADS, t), F32),
        ],
        compiler_params=pltpu.CompilerParams(
            dimension_semantics=("arbitrary",), vmem_limit_bytes=VMEM_LIMIT),
        name="dsa_proj",
    )(x2, w1, wqit, wwit, wukbd, g)


def _attn_kernel(qlat_ref, c_ref, ki_ref, qit_ref, wit_ref, wuvbd_ref, o_ref,
                 st_ref, dm_ref, lg_ref, acc_ref, mx_ref, sm_ref):
    j = pl.program_id(1)
    nkc = (j + 2) // 2
    q0 = j * Q_TILE
    neg_inf = jnp.float32(-jnp.inf)
    pos_inf = jnp.float32(jnp.inf)

    qi_all = jnp.concatenate(
        [qit_ref[h * IDX_DIM:(h + 1) * IDX_DIM, :] for h in range(IDX_HEADS)], axis=1)
    wit = wit_ref[...]
    qpos = q0 + lax.broadcasted_iota(jnp.int32, (K_CHUNK, Q_TILE), 1)
    krow = lax.broadcasted_iota(jnp.int32, (K_CHUNK, Q_TILE), 0)

    def score_body(kc, carry):
        mn, mxv = carry
        k0 = pl.multiple_of(kc * K_CHUNK, K_CHUNK)
        s_all = jnp.dot(ki_ref[0, pl.ds(k0, K_CHUNK), :], qi_all,
                        preferred_element_type=F32)
        s = jnp.zeros((K_CHUNK, Q_TILE), F32)
        for h in range(IDX_HEADS):
            s = s + wit[h:h + 1, :] * jnp.maximum(s_all[:, h * Q_TILE:(h + 1) * Q_TILE], 0.0)
        valid = (k0 + krow) <= qpos
        st_ref[pl.ds(k0, K_CHUNK), :] = jnp.where(valid, s, neg_inf)
        mn = jnp.minimum(mn, jnp.min(jnp.where(valid, s, pos_inf).reshape(-1, 8, Q_TILE), axis=0))
        mxv = jnp.maximum(mxv, jnp.max(jnp.where(valid, s, neg_inf).reshape(-1, 8, Q_TILE), axis=0))
        return mn, mxv

    mn, mxv = lax.fori_loop(
        0, nkc, score_body,
        (jnp.full((8, Q_TILE), pos_inf, F32), jnp.full((8, Q_TILE), neg_inf, F32)))
    rowmin = jnp.broadcast_to(jnp.min(mn, axis=0, keepdims=True), (8, Q_TILE))
    rowmax = jnp.broadcast_to(jnp.max(mxv, axis=0, keepdims=True), (8, Q_TILE))

    def count_ge(thr):
        def body(kc, acc):
            k0 = pl.multiple_of(kc * K_CHUNK, K_CHUNK)
            blk = st_ref[pl.ds(k0, K_CHUNK), :].reshape(-1, 8, Q_TILE)
            return acc + jnp.sum(jnp.where(blk >= thr[None], 1.0, 0.0), axis=0)
        part = lax.fori_loop(0, nkc, body, jnp.zeros((8, Q_TILE), F32))
        return jnp.broadcast_to(jnp.sum(part, axis=0, keepdims=True), (8, Q_TILE))

    topk_f = jnp.float32(TOPK)
    n_valid = (q0 + lax.broadcasted_iota(jnp.int32, (8, Q_TILE), 1) + 1).astype(F32)

    def search_cond(state):
        it, flag = state[0], state[1]
        return jnp.logical_and(flag > 0.0, it < MAX_SEARCH_ITERS)

    def search_body(state):
        it, _, lo, hi, clo, chi, tie = state
        active = jnp.logical_and(clo > topk_f, tie == 0.0)
        first = it == 0
        mid = jnp.where(first, rowmax, lo * 0.5 + hi * 0.5)
        c = count_ge(mid)
        stalled = jnp.logical_and(jnp.logical_or(mid <= lo, mid >= hi), jnp.logical_not(first))
        ge = c >= topk_f
        live = jnp.logical_and(active, jnp.logical_not(stalled))
        upd_lo = jnp.logical_and(live, ge)
        upd_hi = jnp.logical_and(live, jnp.logical_not(ge))
        lo = jnp.where(upd_lo, mid, lo)
        clo = jnp.where(upd_lo, c, clo)
        hi = jnp.where(upd_hi, mid, hi)
        chi = jnp.where(upd_hi, c, chi)
        tie = jnp.where(jnp.logical_and(active, stalled), 1.0, tie)
        still = jnp.logical_and(clo > topk_f, tie == 0.0)
        flag = jnp.max(jnp.where(still, 1.0, 0.0))
        return it + 1, flag, lo, hi, clo, chi, tie

    init = (jnp.int32(0), jnp.max(jnp.where(n_valid > topk_f, 1.0, 0.0)),
            rowmin, jnp.full((8, Q_TILE), pos_inf, F32), n_valid,
            jnp.zeros((8, Q_TILE), F32), jnp.zeros((8, Q_TILE), F32))
    _, _, lo, hi, _, chi, tie = lax.while_loop(search_cond, search_body, init)

    lo_r, hi_r = lo[0:1, :], hi[0:1, :]
    tie_r = tie[0:1, :] > 0.0
    need_r = topk_f - chi[0:1, :]
    thr_r = jnp.where(tie_r, hi_r, lo_r)
    tri = (lax.broadcasted_iota(jnp.int32, (K_CHUNK, K_CHUNK), 1)
           < lax.broadcasted_iota(jnp.int32, (K_CHUNK, K_CHUNK), 0)).astype(BF16)

    def mask_body(kc, run):
        k0 = pl.multiple_of(kc * K_CHUNK, K_CHUNK)
        blk = st_ref[pl.ds(k0, K_CHUNK), :]
        sel = blk >= thr_r
        eq = jnp.logical_and(tie_r, blk == lo_r)
        eq_f = jnp.where(eq, 1.0, 0.0)
        rank = jnp.dot(tri, eq_f.astype(BF16), preferred_element_type=F32) + run
        sel = jnp.logical_or(sel, jnp.logical_and(eq, rank < need_r))
        dist = ((k0 + krow) - qpos).astype(F32)
        dm_ref[kc] = jnp.where(sel, dist, MASK_BIAS).T
        return run + jnp.sum(eq_f, axis=0, keepdims=True)

    lax.fori_loop(0, nkc, mask_body, jnp.zeros((1, Q_TILE), F32))

    rows = HEAD_GROUP * Q_TILE
    for hg in range(N_HEAD_GROUPS):
        ql = qlat_ref[0, hg * HEAD_GROUP:(hg + 1) * HEAD_GROUP].reshape(rows, KV_LATENT)
        slopes = [2.0 ** (-8.0 * (hg * HEAD_GROUP + hh + 1) / N_HEADS) for hh in range(HEAD_GROUP)]

        mx_ref[...] = jnp.full((rows, 128), neg_inf, F32)

        def logit_body(kc, _):
            k0 = pl.multiple_of(kc * K_CHUNK, K_CHUNK)
            lg = _nt_dot(ql, c_ref[0, pl.ds(k0, K_CHUNK), :])
            dmk = dm_ref[kc]
            parts = [lg[hh * Q_TILE:(hh + 1) * Q_TILE] + slopes[hh] * dmk for hh in range(HEAD_GROUP)]
            lg = jnp.concatenate(parts, axis=0)
            lg_ref[kc] = lg
            mx_ref[...] = jnp.maximum(mx_ref[...], jnp.maximum(lg[:, :128], lg[:, 128:]))
            return 0

        lax.fori_loop(0, nkc, logit_body, 0)
        m = jnp.max(mx_ref[...], axis=-1, keepdims=True)

        acc_ref[...] = jnp.zeros((rows, KV_LATENT), F32)
        sm_ref[...] = jnp.zeros((rows, 128), F32)

        def pv_body(kc, _):
            k0 = pl.multiple_of(kc * K_CHUNK, K_CHUNK)
            p = jnp.exp(lg_ref[kc] - m)
            sm_ref[...] += p[:, :128] + p[:, 128:]
            acc_ref[...] += jnp.dot(p.astype(BF16), c_ref[0, pl.ds(k0, K_CHUNK), :],
                                    preferred_element_type=F32)
            return 0

        lax.fori_loop(0, nkc, pv_body, 0)
        denom = jnp.sum(sm_ref[...], axis=-1, keepdims=True)
        olat = (acc_ref[...] / denom).astype(BF16)
        ocat = jnp.concatenate([olat[hh * Q_TILE:(hh + 1) * Q_TILE] for hh in range(HEAD_GROUP)], axis=1)
        og = jnp.dot(ocat, wuvbd_ref[hg], preferred_element_type=F32)
        o_ref[0, :, hg * 256:(hg + 1) * 256] = og.astype(BF16)


def _attn_call(qlat, c, ki, qit, wit, wuvbd, bsz, seq):
    nq = seq // Q_TILE
    rows = HEAD_GROUP * Q_TILE
    n_chunks = seq // K_CHUNK
    return pl.pallas_call(
        _attn_kernel,
        grid=(bsz, nq),
        in_specs=[
            pl.BlockSpec((1, N_HEADS, Q_TILE, KV_LATENT), lambda b, j: (b, 0, j, 0)),
            pl.BlockSpec((1, seq, KV_LATENT), lambda b, j: (b, 0, 0)),
            pl.BlockSpec((1, seq, IDX_DIM), lambda b, j: (b, 0, 0)),
            pl.BlockSpec((IDX_HEADS * IDX_DIM, Q_TILE), lambda b, j: (0, b * nq + j)),
            pl.BlockSpec((IDX_HEADS, Q_TILE), lambda b, j: (0, b * nq + j)),
            pl.BlockSpec(wuvbd.shape, lambda b, j: (0, 0, 0)),
        ],
        out_specs=pl.BlockSpec((1, Q_TILE, D_MODEL), lambda b, j: (b, j, 0)),
        out_shape=jax.ShapeDtypeStruct((bsz, seq, D_MODEL), BF16),
        scratch_shapes=[
            pltpu.VMEM((seq, Q_TILE), F32),
            pltpu.VMEM((n_chunks, Q_TILE, K_CHUNK), F32),
            pltpu.VMEM((n_chunks, rows, K_CHUNK), F32),
            pltpu.VMEM((rows, KV_LATENT), F32),
            pltpu.VMEM((rows, 128), F32),
            pltpu.VMEM((rows, 128), F32),
        ],
        compiler_params=pltpu.CompilerParams(
            dimension_semantics=("arbitrary", "arbitrary"), vmem_limit_bytes=VMEM_LIMIT),
        name="dsa_attn",
    )(qlat, c, ki, qit, wit, wuvbd)


def _pool_kernel(h_ref, halo_ref, win_ref, wgrp_ref, scale_ref, y_ref):
    i = pl.program_id(1)
    tm = h_ref.shape[1]
    u_main = jnp.dot(h_ref[0].astype(BF16), win_ref[...], preferred_element_type=F32)
    u_halo = jnp.dot(halo_ref[0].astype(BF16), win_ref[...], preferred_element_type=F32)
    u_halo = u_halo * jnp.where(i > 0, 1.0, 0.0)
    u = jnp.concatenate([u_halo, u_main], axis=0)
    pos = i * tm + lax.broadcasted_iota(jnp.int32, (tm, 1), 0)
    for g, window in enumerate(POOL_WINDOWS):
        ug = u[:, g * POOL_GROUP_DIM:(g + 1) * POOL_GROUP_DIM]
        s = ug
        shift = 1
        while shift < window:
            s = s + pltpu.roll(s, shift, axis=0)
            shift *= 2
        cnt = jnp.minimum(pos + 1, window).astype(F32)
        pooled = (s[POOL_HALO:] / cnt - ug[POOL_HALO:]).astype(BF16)
        yg = jnp.dot(pooled, wgrp_ref[g], preferred_element_type=F32)
        yg = yg * scale_ref[:, g * POOL_GROUP_DIM:(g + 1) * POOL_GROUP_DIM]
        y_ref[0, :, g * POOL_GROUP_DIM:(g + 1) * POOL_GROUP_DIM] = yg.astype(BF16)


def _pool_call(h3, win, wgrp, scale, tm):
    bsz, seq, _ = h3.shape
    halo_per = tm // POOL_HALO
    return pl.pallas_call(
        _pool_kernel,
        grid=(bsz, seq // tm),
        in_specs=[
            pl.BlockSpec((1, tm, D_MODEL), lambda b, i: (b, i, 0)),
            pl.BlockSpec((1, POOL_HALO, D_MODEL), lambda b, i: (b, jnp.maximum(i * halo_per - 1, 0), 0)),
            pl.BlockSpec(win.shape, lambda b, i: (0, 0)),
            pl.BlockSpec(wgrp.shape, lambda b, i: (0, 0, 0)),
            pl.BlockSpec(scale.shape, lambda b, i: (0, 0)),
        ],
        out_specs=pl.BlockSpec((1, tm, D_MODEL), lambda b, i: (b, i, 0)),
        out_shape=jax.ShapeDtypeStruct((bsz, seq, D_MODEL), BF16),
        compiler_params=pltpu.CompilerParams(
            dimension_semantics=("arbitrary", "arbitrary"), vmem_limit_bytes=VMEM_LIMIT),
        name="pool_mix",
    )(h3, h3, win, wgrp, scale)


def _post_kernel(h_ref, mi_ref, wmo_ref, g1_ref, b1_ref, wg_ref, wu_ref, wd_ref, g2_ref, b2_ref,
                 out_ref, h1_ref, hb_ref, acc_ref):
    mix = jnp.dot(mi_ref[...], wmo_ref[...], preferred_element_type=F32)
    h1 = _layer_norm(DEEPNORM_ALPHA * h_ref[...] + mix, g1_ref[...], b1_ref[...])
    h1_ref[...] = h1
    hb_ref[...] = h1.astype(BF16)
    acc_ref[...] = jnp.zeros(acc_ref.shape, F32)

    def ff_body(f, _):
        hb = hb_ref[...]
        gate = jnp.dot(hb, wg_ref[f], preferred_element_type=F32)
        up = jnp.dot(hb, wu_ref[f], preferred_element_type=F32)
        act = (gate * (1.0 / (1.0 + jnp.exp(-gate))) * up).astype(BF16)
        acc_ref[...] += jnp.dot(act, wd_ref[f], preferred_element_type=F32)
        return 0

    lax.fori_loop(0, N_FF_CHUNKS, ff_body, 0)
    out_ref[...] = _layer_norm(DEEPNORM_ALPHA * h1_ref[...] + acc_ref[...], g2_ref[...], b2_ref[...])


def _post_call(h2, mi2, wmo, g1, b1, wg3, wu3, wd3, g2, b2, tm):
    t = h2.shape[0]
    const2 = lambda i: (0, 0)
    const3 = lambda i: (0, 0, 0)
    resident = dict(pipeline_mode=pl.Buffered(1))
    return pl.pallas_call(
        _post_kernel,
        grid=(t // tm,),
        in_specs=[
            pl.BlockSpec((tm, D_MODEL), lambda i: (i, 0)),
            pl.BlockSpec((tm, D_MODEL), lambda i: (i, 0)),
            pl.BlockSpec(wmo.shape, const2, **resident),
            pl.BlockSpec(g1.shape, const2),
            pl.BlockSpec(b1.shape, const2),
            pl.BlockSpec(wg3.shape, const3, **resident),
            pl.BlockSpec(wu3.shape, const3, **resident),
            pl.BlockSpec(wd3.shape, const3, **resident),
            pl.BlockSpec(g2.shape, const2),
            pl.BlockSpec(b2.shape, const2),
        ],
        out_specs=pl.BlockSpec((tm, D_MODEL), lambda i: (i, 0)),
        out_shape=jax.ShapeDtypeStruct((t, D_MODEL), F32),
        scratch_shapes=[
            pltpu.VMEM((tm, D_MODEL), F32),
            pltpu.VMEM((tm, D_MODEL), BF16),
            pltpu.VMEM((tm, D_MODEL), F32),
        ],
        compiler_params=pltpu.CompilerParams(
            dimension_semantics=("arbitrary",), vmem_limit_bytes=VMEM_LIMIT),
        name="post_ffn",
    )(h2, mi2, wmo, g1, b1, wg3, wu3, wd3, g2, b2)


def _ffn_weights(w_gu, w_down):
    wg3 = w_gu[:, :D_FF].reshape(D_MODEL, N_FF_CHUNKS, FF_CHUNK).transpose(1, 0, 2).astype(BF16)
    wu3 = w_gu[:, D_FF:].reshape(D_MODEL, N_FF_CHUNKS, FF_CHUNK).transpose(1, 0, 2).astype(BF16)
    wd3 = w_down.reshape(N_FF_CHUNKS, FF_CHUNK, D_MODEL).astype(BF16)
    return wg3, wu3, wd3


def kernel(x, a_w_in, a_w_uk, a_w_uv, a_kv_norm_g, a_w_o, b_w_in, b_w_grp, b_scale, b_w_o,
           f_w_gu, f_w_down, ln_mix_g, ln_mix_b, ln_ffn_g, ln_ffn_b):
    bsz, seq, _ = x.shape
    t = bsz * seq
    x2 = x.reshape(t, D_MODEL)
    row = lambda v: v.reshape(1, -1)

    w_in = a_w_in[0]
    o_c = D_MODEL + KV_LATENT
    o_qi = o_c + IDX_HEADS * IDX_DIM
    o_ki = o_qi + IDX_DIM
    w1 = jnp.concatenate([w_in[:, :o_c], w_in[:, o_qi:o_ki]], axis=1).astype(BF16)
    wqit = w_in[:, o_c:o_qi].T.astype(BF16)
    wwit = w_in[:, o_ki:].T.astype(BF16)
    eye = jnp.eye(HEAD_GROUP, dtype=F32)
    ukt = jnp.swapaxes(a_w_uk[0], 1, 2).reshape(N_HEAD_GROUPS, HEAD_GROUP, HEAD_DIM, KV_LATENT)
    wukbd = jnp.einsum('ghdc,hk->ghdkc', ukt, eye).reshape(
        N_HEAD_GROUPS, HEAD_GROUP * HEAD_DIM, HEAD_GROUP * KV_LATENT).astype(BF16)
    uv = a_w_uv[0].reshape(N_HEAD_GROUPS, HEAD_GROUP, KV_LATENT, HEAD_DIM)
    wuvbd = jnp.einsum('ghcd,hk->ghckd', uv, eye).reshape(
        N_HEAD_GROUPS, HEAD_GROUP * KV_LATENT, HEAD_GROUP * HEAD_DIM).astype(BF16)

    qlat, c, ki, qit, wit = _proj_call(x2, w1, wqit, wwit, wukbd, row(a_kv_norm_g[0]), bsz, seq, 512)
    o = _attn_call(qlat, c, ki, qit, wit, wuvbd, bsz, seq)
    wg3, wu3, wd3 = _ffn_weights(f_w_gu[0], f_w_down[0])
    h = _post_call(x2, o.reshape(t, D_MODEL), a_w_o[0].astype(BF16),
                   row(ln_mix_g[0]), row(ln_mix_b[0]), wg3, wu3, wd3,
                   row(ln_ffn_g[0]), row(ln_ffn_b[0]), 512)

    y = _pool_call(h.reshape(bsz, seq, D_MODEL), b_w_in[0].astype(BF16), b_w_grp[0].astype(BF16),
                   row(b_scale[0]), 512)
    wg3, wu3, wd3 = _ffn_weights(f_w_gu[1], f_w_down[1])
    h = _post_call(h, y.reshape(t, D_MODEL), b_w_o[0].astype(BF16),
                   row(ln_mix_g[1]), row(ln_mix_b[1]), wg3, wu3, wd3,
                   row(ln_ffn_g[1]), row(ln_ffn_b[1]), 512)
    return h.reshape(bsz, seq, D_MODEL)
```

```python
import functools
import math

import jax
import jax.numpy as jnp
from jax import lax
from jax.experimental import pallas as pl
from jax.experimental.pallas import tpu as pltpu

D_MODEL = 1024
N_HEADS = 16
HEAD_DIM = 64
KV_LATENT = 256
IDX_HEADS = 8
IDX_DIM = 64
TOPK = 256
POOL_WINDOWS = (2, 4, 8, 16)
POOL_GROUP_DIM = 256
D_FF = 2816
DEPTH = 2
DEEPNORM_ALPHA = (2 * DEPTH) ** 0.25
LN_EPS = 1e-5
RMS_EPS = 1e-6
LOG2E = math.log2(math.e)

Q_TILE = 128
K_CHUNK = 256
HEAD_GROUP = 4
N_HEAD_GROUPS = N_HEADS // HEAD_GROUP
ATT_ROWS = N_HEADS * Q_TILE
FF_CHUNK = 256
N_FF_CHUNKS = D_FF // FF_CHUNK
POOL_HALO = 16
MASK_BIAS = -1e30
N_PLAIN_SEARCH = 10
MAX_SNAP_ITERS = 4096

BF16 = jnp.bfloat16
F32 = jnp.float32
VMEM_LIMIT = 56 * 1024 * 1024


def _nt_dot(a, b):
    return lax.dot_general(a, b, (((1,), (1,)), ((), ())), preferred_element_type=F32)


def _layer_norm(z, g, b):
    mu = jnp.mean(z, axis=-1, keepdims=True)
    zc = z - mu
    var = jnp.mean(zc * zc, axis=-1, keepdims=True)
    return zc * lax.rsqrt(var + LN_EPS) * g + b


def _fold_rows(x, op, parts=4):
    x = x.reshape(parts, -1, 8, x.shape[-1])
    y = op(x, axis=1)
    return op(y, axis=0)


def _lane_rep(x, op):
    return jnp.broadcast_to(op(x, axis=0, keepdims=True), x.shape)


def _proj_kernel(x_ref, w1_ref, wqit_ref, wwit_ref, wukbd_ref, g_ref,
                 qlat_ref, c_ref, ki_ref, qit_ref, wit_ref):
    xb = x_ref[...].astype(BF16)
    main = jnp.dot(xb, w1_ref[...], preferred_element_type=F32)
    ckv = main[:, D_MODEL:D_MODEL + KV_LATENT]
    ms = jnp.mean(ckv * ckv, axis=-1, keepdims=True)
    c_ref[0] = (ckv * lax.rsqrt(ms + RMS_EPS) * g_ref[...]).astype(BF16)
    ki_ref[0] = main[:, D_MODEL + KV_LATENT:].astype(BF16)
    qscale = (HEAD_DIM ** -0.5) * LOG2E
    for g4 in range(N_HEAD_GROUPS):
        qg = main[:, g4 * 256:(g4 + 1) * 256].astype(BF16)
        ql = jnp.dot(qg, wukbd_ref[g4], preferred_element_type=F32) * qscale
        for hh in range(HEAD_GROUP):
            qlat_ref[0, g4 * HEAD_GROUP + hh] = ql[:, hh * KV_LATENT:(hh + 1) * KV_LATENT].astype(BF16)
    qit_ref[...] = (_nt_dot(wqit_ref[...], xb) * (IDX_DIM ** -0.5)).astype(BF16)
    wit_ref[...] = _nt_dot(wwit_ref[...], xb) * (IDX_HEADS ** -0.5)


def _proj_call(x2, w1, wqit, wwit, wukbd, g, bsz, seq, tm):
    t = bsz * seq
    per = seq // tm
    return pl.pallas_call(
        _proj_kernel,
        grid=(t // tm,),
        in_specs=[
            pl.BlockSpec((tm, D_MODEL), lambda i: (i, 0)),
            pl.BlockSpec(w1.shape, lambda i: (0, 0)),
            pl.BlockSpec(wqit.shape, lambda i: (0, 0)),
            pl.BlockSpec(wwit.shape, lambda i: (0, 0)),
            pl.BlockSpec(wukbd.shape, lambda i: (0, 0, 0)),
            pl.BlockSpec(g.shape, lambda i: (0, 0)),
        ],
        out_specs=[
            pl.BlockSpec((1, N_HEADS, tm, KV_LATENT), lambda i: (i // per, 0, i % per, 0)),
            pl.BlockSpec((1, tm, KV_LATENT), lambda i: (i // per, i % per, 0)),
            pl.BlockSpec((1, tm, IDX_DIM), lambda i: (i // per, i % per, 0)),
            pl.BlockSpec((IDX_HEADS * IDX_DIM, tm), lambda i: (0, i)),
            pl.BlockSpec((IDX_HEADS, tm), lambda i: (0, i)),
        ],
        out_shape=[
            jax.ShapeDtypeStruct((bsz, N_HEADS, seq, KV_LATENT), BF16),
            jax.ShapeDtypeStruct((bsz, seq, KV_LATENT), BF16),
            jax.ShapeDtypeStruct((bsz, seq, IDX_DIM), BF16),
            jax.ShapeDtypeStruct((IDX_HEADS * IDX_DIM, t), BF16),
            jax.ShapeDtypeStruct((IDX_HEADS, t), F32),
        ],
        compiler_params=pltpu.CompilerParams(
            dimension_semantics=("arbitrary",), vmem_limit_bytes=VMEM_LIMIT),
        name="dsa_proj",
    )(x2, w1, wqit, wwit, wukbd, g)


def _attn_kernel(qlat_ref, c_ref, ki_ref, qit_ref, wit_ref, o_ref,
                 st_ref, dm_ref, lg_ref, acc_ref, mx_ref, sm_ref):
    j = pl.program_id(1)
    nkc = (j + 2) // 2
    npair = (nkc + 1) // 2
    q0 = j * Q_TILE
    neg_inf = jnp.float32(-jnp.inf)
    pos_inf = jnp.float32(jnp.inf)
    vshape = (8, Q_TILE)

    qi_all = jnp.concatenate(
        [qit_ref[h * IDX_DIM:(h + 1) * IDX_DIM, :] for h in range(IDX_HEADS)], axis=1)
    wit = wit_ref[...]
    qpos = q0 + lax.broadcasted_iota(jnp.int32, (K_CHUNK, Q_TILE), 1)
    krow = lax.broadcasted_iota(jnp.int32, (K_CHUNK, Q_TILE), 0)

    def score_body(kc, carry):
        mn, mxv = carry
        k0 = pl.multiple_of(kc * K_CHUNK, K_CHUNK)
        s_all = jnp.dot(ki_ref[0, pl.ds(k0, K_CHUNK), :], qi_all,
                        preferred_element_type=F32)
        s = wit[0:1, :] * jnp.maximum(s_all[:, 0:Q_TILE], 0.0)
        for h in range(1, IDX_HEADS):
            s = s + wit[h:h + 1, :] * jnp.maximum(s_all[:, h * Q_TILE:(h + 1) * Q_TILE], 0.0)
        valid = (k0 + krow) <= qpos
        st_ref[pl.ds(k0, K_CHUNK), :] = jnp.where(valid, s, neg_inf)
        mn = jnp.minimum(mn, _fold_rows(jnp.where(valid, s, pos_inf), jnp.min))
        mxv = jnp.maximum(mxv, _fold_rows(jnp.where(valid, s, neg_inf), jnp.max))
        return mn, mxv

    mn, mxv = lax.fori_loop(
        0, nkc, score_body, (jnp.full(vshape, pos_inf, F32), jnp.full(vshape, neg_inf, F32)))
    rowmin = _lane_rep(mn, jnp.min)
    rowmax = _lane_rep(mxv, jnp.max)

    @pl.when(nkc % 2 == 1)
    def _():
        k0 = pl.multiple_of(nkc * K_CHUNK, K_CHUNK)
        st_ref[pl.ds(k0, K_CHUNK), :] = jnp.full((K_CHUNK, Q_TILE), neg_inf, F32)

    topk_f = jnp.float32(TOPK)

    def pair_block(kp):
        k0 = pl.multiple_of(kp * (2 * K_CHUNK), 2 * K_CHUNK)
        return st_ref[pl.ds(k0, 2 * K_CHUNK), :]

    def count_ge(thr):
        def body(kp, acc):
            return acc + _fold_rows(jnp.where(pair_block(kp) >= thr[0:1, :], 1.0, 0.0), jnp.sum)
        return _lane_rep(lax.fori_loop(0, npair, body, jnp.zeros(vshape, F32)), jnp.sum)

    def count_snap(thr):
        def body(kp, carry):
            cnt, amin, bmx = carry
            blk = pair_block(kp)
            ge = blk >= thr[0:1, :]
            cnt = cnt + _fold_rows(jnp.where(ge, 1.0, 0.0), jnp.sum)
            amin = jnp.minimum(amin, _fold_rows(jnp.where(ge, blk, pos_inf), jnp.min))
            bmx = jnp.maximum(bmx, _fold_rows(jnp.where(ge, neg_inf, blk), jnp.max))
            return cnt, amin, bmx
        cnt, amin, bmx = lax.fori_loop(
            0, npair, body,
            (jnp.zeros(vshape, F32), jnp.full(vshape, pos_inf, F32), jnp.full(vshape, neg_inf, F32)))
        return _lane_rep(cnt, jnp.sum), _lane_rep(amin, jnp.min), _lane_rep(bmx, jnp.max)

    def unsettled(clo, bmin, bmax):
        return jnp.logical_and(clo > topk_f, bmin < bmax)

    def probe_point(lo, hi, bmin, bmax):
        a = jnp.maximum(lo, bmin)
        b = jnp.minimum(hi, bmax)
        mid = a * 0.5 + b * 0.5
        return jnp.where(mid <= lo, b, mid)

    def plain_body(i, state):
        lo, hi, clo, chi, bmin, bmax = state
        active = unsettled(clo, bmin, bmax)
        mid = jnp.where(i == 0, rowmax, probe_point(lo, hi, bmin, bmax))
        c = count_ge(mid)
        up = jnp.logical_and(active, c >= topk_f)
        dn = jnp.logical_and(active, c < topk_f)
        return (jnp.where(up, mid, lo), jnp.where(dn, mid, hi),
                jnp.where(up, c, clo), jnp.where(dn, c, chi), bmin, bmax)

    n_valid = (q0 + lax.broadcasted_iota(jnp.int32, vshape, 1) + 1).astype(F32)
    state = (rowmin, jnp.full(vshape, pos_inf, F32), n_valid, jnp.zeros(vshape, F32), rowmin, rowmax)
    state = lax.fori_loop(0, N_PLAIN_SEARCH, plain_body, state)

    def snap_flag(clo, bmin, bmax):
        return jnp.max(jnp.where(unsettled(clo, bmin, bmax), 1.0, 0.0))

    def snap_cond(carry):
        it, flag = carry[0], carry[1]
        return jnp.logical_and(flag > 0.0, it < MAX_SNAP_ITERS)

    def snap_body(carry):
        it, _, lo, hi, clo, chi, bmin, bmax = carry
        active = unsettled(clo, bmin, bmax)
        mid = probe_point(lo, hi, bmin, bmax)
        c, amin, bmx = count_snap(mid)
        up = jnp.logical_and(active, c >= topk_f)
        dn = jnp.logical_and(active, c < topk_f)
        lo, clo, bmin = jnp.where(up, mid, lo), jnp.where(up, c, clo), jnp.where(up, amin, bmin)
        hi, chi, bmax = jnp.where(dn, mid, hi), jnp.where(dn, c, chi), jnp.where(dn, bmx, bmax)
        return it + 1, snap_flag(clo, bmin, bmax), lo, hi, clo, chi, bmin, bmax

    lo, hi, clo, chi, bmin, bmax = state
    carry = lax.while_loop(snap_cond, snap_body,
                           (jnp.int32(0), snap_flag(clo, bmin, bmax), lo, hi, clo, chi, bmin, bmax))
    _, _, lo, hi, clo, chi, bmin, _ = carry

    tie_r = clo[0:1, :] > topk_f
    thr_r = jnp.where(tie_r, hi[0:1, :], lo[0:1, :])
    tv_r = bmin[0:1, :]
    need_r = topk_f - chi[0:1, :]
    tri = (lax.broadcasted_iota(jnp.int32, (K_CHUNK, K_CHUNK), 1)
           < lax.broadcasted_iota(jnp.int32, (K_CHUNK, K_CHUNK), 0)).astype(BF16)

    def mask_body(kc, run):
        k0 = pl.multiple_of(kc * K_CHUNK, K_CHUNK)
        blk = st_ref[pl.ds(k0, K_CHUNK), :]
        sel = blk >= thr_r
        eq = jnp.logical_and(tie_r, blk == tv_r)
        eq_f = jnp.where(eq, 1.0, 0.0)
        rank = jnp.dot(tri, eq_f.astype(BF16), preferred_element_type=F32) + run
        sel = jnp.logical_or(sel, jnp.logical_and(eq, rank < need_r))
        dist = ((k0 + krow) - qpos).astype(F32)
        dm_ref[kc] = jnp.where(sel, dist, MASK_BIAS).T
        return run + jnp.sum(eq_f, axis=0, keepdims=True)

    lax.fori_loop(0, nkc, mask_body, jnp.zeros((1, Q_TILE), F32))

    ql = qlat_ref[0].reshape(ATT_ROWS, KV_LATENT)
    slopes = [LOG2E * 2.0 ** (-8.0 * (h + 1) / N_HEADS) for h in range(N_HEADS)]
    mx_ref[...] = jnp.full(mx_ref.shape, neg_inf, F32)

    def logit_body(kc, _):
        k0 = pl.multiple_of(kc * K_CHUNK, K_CHUNK)
        lg = _nt_dot(ql, c_ref[0, pl.ds(k0, K_CHUNK), :])
        dmk = dm_ref[kc]
        for h in range(N_HEADS):
            rs = slice(h * Q_TILE, (h + 1) * Q_TILE)
            lgh = lg[rs] + slopes[h] * dmk
            lg_ref[kc, rs, :] = lgh
            mx_ref[rs, :] = jnp.maximum(mx_ref[rs, :], jnp.maximum(lgh[:, :128], lgh[:, 128:]))
        return 0

    lax.fori_loop(0, nkc, logit_body, 0)
    m = jnp.max(mx_ref[...], axis=-1, keepdims=True)

    acc_ref[...] = jnp.zeros(acc_ref.shape, F32)
    sm_ref[...] = jnp.zeros(sm_ref.shape, F32)

    def pv_body(kc, _):
        k0 = pl.multiple_of(kc * K_CHUNK, K_CHUNK)
        p = jnp.exp2(lg_ref[kc] - m)
        sm_ref[...] += p[:, :128] + p[:, 128:]
        acc_ref[...] += jnp.dot(p.astype(BF16), c_ref[0, pl.ds(k0, K_CHUNK), :],
                                preferred_element_type=F32)
        return 0

    lax.fori_loop(0, nkc, pv_body, 0)
    inv = 1.0 / jnp.sum(sm_ref[...], axis=-1, keepdims=True)
    for h in range(N_HEADS):
        rs = slice(h * Q_TILE, (h + 1) * Q_TILE)
        o_ref[0, :, h * KV_LATENT:(h + 1) * KV_LATENT] = (acc_ref[rs, :] * inv[rs]).astype(BF16)


def _attn_call(qlat, c, ki, qit, wit, bsz, seq):
    nq = seq // Q_TILE
    n_chunks = seq // K_CHUNK
    return pl.pallas_call(
        _attn_kernel,
        grid=(bsz, nq),
        in_specs=[
            pl.BlockSpec((1, N_HEADS, Q_TILE, KV_LATENT), lambda b, j: (b, 0, j, 0)),
            pl.BlockSpec((1, seq, KV_LATENT), lambda b, j: (b, 0, 0)),
            pl.BlockSpec((1, seq, IDX_DIM), lambda b, j: (b, 0, 0)),
            pl.BlockSpec((IDX_HEADS * IDX_DIM, Q_TILE), lambda b, j: (0, b * nq + j)),
            pl.BlockSpec((IDX_HEADS, Q_TILE), lambda b, j: (0, b * nq + j)),
        ],
        out_specs=pl.BlockSpec((1, Q_TILE, N_HEADS * KV_LATENT), lambda b, j: (b, j, 0)),
        out_shape=jax.ShapeDtypeStruct((bsz, seq, N_HEADS * KV_LATENT), BF16),
        scratch_shapes=[
            pltpu.VMEM((seq, Q_TILE), F32),
            pltpu.VMEM((n_chunks, Q_TILE, K_CHUNK), F32),
            pltpu.VMEM((n_chunks, ATT_ROWS, K_CHUNK), F32),
            pltpu.VMEM((ATT_ROWS, KV_LATENT), F32),
            pltpu.VMEM((ATT_ROWS, 128), F32),
            pltpu.VMEM((ATT_ROWS, 128), F32),
        ],
        compiler_params=pltpu.CompilerParams(
            dimension_semantics=("arbitrary", "arbitrary"), vmem_limit_bytes=VMEM_LIMIT),
        name="dsa_attn",
    )(qlat, c, ki, qit, wit)


def _pool_kernel(h_ref, halo_ref, win_ref, wgrp_ref, scale_ref, y_ref):
    i = pl.program_id(1)
    tm = h_ref.shape[1]
    u_main = jnp.dot(h_ref[0].astype(BF16), win_ref[...], preferred_element_type=F32)
    u_halo = jnp.dot(halo_ref[0].astype(BF16), win_ref[...], preferred_element_type=F32)
    u_halo = u_halo * jnp.where(i > 0, 1.0, 0.0)
    u = jnp.concatenate([u_halo, u_main], axis=0)
    pos = i * tm + lax.broadcasted_iota(jnp.int32, (tm, 1), 0)
    for g, window in enumerate(POOL_WINDOWS):
        ug = u[:, g * POOL_GROUP_DIM:(g + 1) * POOL_GROUP_DIM]
        s = ug
        shift = 1
        while shift < window:
            s = s + pltpu.roll(s, shift, axis=0)
            shift *= 2
        cnt = jnp.minimum(pos + 1, window).astype(F32)
        pooled = (s[POOL_HALO:] / cnt - ug[POOL_HALO:]).astype(BF16)
        yg = jnp.dot(pooled, wgrp_ref[g], preferred_element_type=F32)
        yg = yg * scale_ref[:, g * POOL_GROUP_DIM:(g + 1) * POOL_GROUP_DIM]
        y_ref[0, :, g * POOL_GROUP_DIM:(g + 1) * POOL_GROUP_DIM] = yg.astype(BF16)


def _pool_call(h3, win, wgrp, scale, tm):
    bsz, seq, _ = h3.shape
    halo_per = tm // POOL_HALO
    return pl.pallas_call(
        _pool_kernel,
        grid=(bsz, seq // tm),
        in_specs=[
            pl.BlockSpec((1, tm, D_MODEL), lambda b, i: (b, i, 0)),
            pl.BlockSpec((1, POOL_HALO, D_MODEL), lambda b, i: (b, jnp.maximum(i * halo_per - 1, 0), 0)),
            pl.BlockSpec(win.shape, lambda b, i: (0, 0)),
            pl.BlockSpec(wgrp.shape, lambda b, i: (0, 0, 0)),
            pl.BlockSpec(scale.shape, lambda b, i: (0, 0)),
        ],
        out_specs=pl.BlockSpec((1, tm, D_MODEL), lambda b, i: (b, i, 0)),
        out_shape=jax.ShapeDtypeStruct((bsz, seq, D_MODEL), BF16),
        compiler_params=pltpu.CompilerParams(
            dimension_semantics=("arbitrary", "arbitrary"), vmem_limit_bytes=VMEM_LIMIT),
        name="pool_mix",
    )(h3, h3, win, wgrp, scale)


def _post_kernel(*refs, with_uv):
    if with_uv:
        (h_ref, mi_ref, wuv_ref, wmo_ref, g1_ref, b1_ref, wg_ref, wu_ref, wd_ref, g2_ref, b2_ref,
         out_ref, h1_ref, hb_ref, acc_ref) = refs
        group = HEAD_GROUP * KV_LATENT
        mi = jnp.concatenate(
            [jnp.dot(mi_ref[:, g * group:(g + 1) * group], wuv_ref[g],
                     preferred_element_type=F32).astype(BF16) for g in range(N_HEAD_GROUPS)], axis=1)
    else:
        (h_ref, mi_ref, wmo_ref, g1_ref, b1_ref, wg_ref, wu_ref, wd_ref, g2_ref, b2_ref,
         out_ref, h1_ref, hb_ref, acc_ref) = refs
        mi = mi_ref[...]
    mix = jnp.dot(mi, wmo_ref[...], preferred_element_type=F32)
    h1 = _layer_norm(DEEPNORM_ALPHA * h_ref[...] + mix, g1_ref[...], b1_ref[...])
    h1_ref[...] = h1
    hb_ref[...] = h1.astype(BF16)
    acc_ref[...] = jnp.zeros(acc_ref.shape, F32)

    def ff_body(f, _):
        hb = hb_ref[...]
        gate = jnp.dot(hb, wg_ref[f], preferred_element_type=F32)
        up = jnp.dot(hb, wu_ref[f], preferred_element_type=F32)
        act = (gate * (1.0 / (1.0 + jnp.exp(-gate))) * up).astype(BF16)
        acc_ref[...] += jnp.dot(act, wd_ref[f], preferred_element_type=F32)
        return 0

    lax.fori_loop(0, N_FF_CHUNKS, ff_body, 0)
    out_ref[...] = _layer_norm(DEEPNORM_ALPHA * h1_ref[...] + acc_ref[...], g2_ref[...], b2_ref[...])


def _post_call(h2, mi2, wuv, wmo, g1, b1, wg3, wu3, wd3, g2, b2, tm):
    t = h2.shape[0]
    const2 = lambda i: (0, 0)
    const3 = lambda i: (0, 0, 0)
    resident = dict(pipeline_mode=pl.Buffered(1))
    with_uv = wuv is not None
    in_specs = [
        pl.BlockSpec((tm, D_MODEL), lambda i: (i, 0)),
        pl.BlockSpec((tm, mi2.shape[1]), lambda i: (i, 0)),
    ]
    operands = [h2, mi2]
    if with_uv:
        in_specs.append(pl.BlockSpec(wuv.shape, const3, **resident))
        operands.append(wuv)
    in_specs += [
        pl.BlockSpec(wmo.shape, const2, **resident),
        pl.BlockSpec(g1.shape, const2),
        pl.BlockSpec(b1.shape, const2),
        pl.BlockSpec(wg3.shape, const3, **resident),
        pl.BlockSpec(wu3.shape, const3, **resident),
        pl.BlockSpec(wd3.shape, const3, **resident),
        pl.BlockSpec(g2.shape, const2),
        pl.BlockSpec(b2.shape, const2),
    ]
    operands += [wmo, g1, b1, wg3, wu3, wd3, g2, b2]
    return pl.pallas_call(
        functools.partial(_post_kernel, with_uv=with_uv),
        grid=(t // tm,),
        in_specs=in_specs,
        out_specs=pl.BlockSpec((tm, D_MODEL), lambda i: (i, 0)),
        out_shape=jax.ShapeDtypeStruct((t, D_MODEL), F32),
        scratch_shapes=[
            pltpu.VMEM((tm, D_MODEL), F32),
            pltpu.VMEM((tm, D_MODEL), BF16),
            pltpu.VMEM((tm, D_MODEL), F32),
        ],
        compiler_params=pltpu.CompilerParams(
            dimension_semantics=("arbitrary",), vmem_limit_bytes=VMEM_LIMIT),
        name="post_ffn_uv" if with_uv else "post_ffn",
    )(*operands)


def _ffn_weights(w_gu, w_down):
    wg3 = w_gu[:, :D_FF].reshape(D_MODEL, N_FF_CHUNKS, FF_CHUNK).transpose(1, 0, 2).astype(BF16)
    wu3 = w_gu[:, D_FF:].reshape(D_MODEL, N_FF_CHUNKS, FF_CHUNK).transpose(1, 0, 2).astype(BF16)
    wd3 = w_down.reshape(N_FF_CHUNKS, FF_CHUNK, D_MODEL).astype(BF16)
    return wg3, wu3, wd3


def kernel(x, a_w_in, a_w_uk, a_w_uv, a_kv_norm_g, a_w_o, b_w_in, b_w_grp, b_scale, b_w_o,
           f_w_gu, f_w_down, ln_mix_g, ln_mix_b, ln_ffn_g, ln_ffn_b):
    bsz, seq, _ = x.shape
    t = bsz * seq
    x2 = x.reshape(t, D_MODEL)
    row = lambda v: v.reshape(1, -1)

    w_in = a_w_in[0]
    o_c = D_MODEL + KV_LATENT
    o_qi = o_c + IDX_HEADS * IDX_DIM
    o_ki = o_qi + IDX_DIM
    w1 = jnp.concatenate([w_in[:, :o_c], w_in[:, o_qi:o_ki]], axis=1).astype(BF16)
    wqit = w_in[:, o_c:o_qi].T.astype(BF16)
    wwit = w_in[:, o_ki:].T.astype(BF16)
    eye = jnp.eye(HEAD_GROUP, dtype=F32)
    ukt = jnp.swapaxes(a_w_uk[0], 1, 2).reshape(N_HEAD_GROUPS, HEAD_GROUP, HEAD_DIM, KV_LATENT)
    wukbd = jnp.einsum('ghdc,hk->ghdkc', ukt, eye).reshape(
        N_HEAD_GROUPS, HEAD_GROUP * HEAD_DIM, HEAD_GROUP * KV_LATENT).astype(BF16)
    uv = a_w_uv[0].reshape(N_HEAD_GROUPS, HEAD_GROUP, KV_LATENT, HEAD_DIM)
    wuvbd = jnp.einsum('ghcd,hk->ghckd', uv, eye).reshape(
        N_HEAD_GROUPS, HEAD_GROUP * KV_LATENT, HEAD_GROUP * HEAD_DIM).astype(BF16)

    qlat, c, ki, qit, wit = _proj_call(x2, w1, wqit, wwit, wukbd, row(a_kv_norm_g[0]), bsz, seq, 512)
    olat = _attn_call(qlat, c, ki, qit, wit, bsz, seq)
    wg3, wu3, wd3 = _ffn_weights(f_w_gu[0], f_w_down[0])
    h = _post_call(x2, olat.reshape(t, N_HEADS * KV_LATENT), wuvbd, a_w_o[0].astype(BF16),
                   row(ln_mix_g[0]), row(ln_mix_b[0]), wg3, wu3, wd3,
                   row(ln_ffn_g[0]), row(ln_ffn_b[0]), 512)

    y = _pool_call(h.reshape(bsz, seq, D_MODEL), b_w_in[0].astype(BF16), b_w_grp[0].astype(BF16),
                   row(b_scale[0]), 512)
    wg3, wu3, wd3 = _ffn_weights(f_w_gu[1], f_w_down[1])
    h = _post_call(h, y.reshape(t, D_MODEL), None, b_w_o[0].astype(BF16),
                   row(ln_mix_g[1]), row(ln_mix_b[1]), wg3, wu3, wd3,
                   row(ln_ffn_g[1]), row(ln_ffn_b[1]), 512)
    return h.reshape(bsz, seq, D_MODEL)
```

```python
import functools
import math

import jax
import jax.numpy as jnp
from jax import lax
from jax.experimental import pallas as pl
from jax.experimental.pallas import tpu as pltpu

D_MODEL = 1024
N_HEADS = 16
HEAD_DIM = 64
KV_LATENT = 256
IDX_HEADS = 8
IDX_DIM = 64
TOPK = 256
POOL_WINDOWS = (2, 4, 8, 16)
POOL_GROUP_DIM = 256
D_FF = 2816
DEPTH = 2
DEEPNORM_ALPHA = (2 * DEPTH) ** 0.25
LN_EPS = 1e-5
RMS_EPS = 1e-6
LOG2E = math.log2(math.e)

Q_TILE = 128
K_CHUNK = 256
HEAD_GROUP = 4
N_HEAD_GROUPS = N_HEADS // HEAD_GROUP
ATT_ROWS = N_HEADS * Q_TILE
FF_CHUNK = 256
N_FF_CHUNKS = D_FF // FF_CHUNK
POOL_HALO = 16
MASK_BIAS = -1e30
N_PLAIN_SEARCH = 12
MAX_SNAP_ITERS = 4096

BF16 = jnp.bfloat16
F32 = jnp.float32
VMEM_LIMIT = 56 * 1024 * 1024


def _nt_dot(a, b):
    return lax.dot_general(a, b, (((1,), (1,)), ((), ())), preferred_element_type=F32)


def _layer_norm(z, g, b):
    mu = jnp.mean(z, axis=-1, keepdims=True)
    zc = z - mu
    var = jnp.mean(zc * zc, axis=-1, keepdims=True)
    return zc * lax.rsqrt(var + LN_EPS) * g + b


def _fold_rows(x, op, parts=4):
    x = x.reshape(parts, -1, 8, x.shape[-1])
    y = op(x, axis=1)
    return op(y, axis=0)


def _lane_rep(x, op):
    return jnp.broadcast_to(op(x, axis=0, keepdims=True), x.shape)


def _proj_kernel(x_ref, w1_ref, wqit_ref, wwit_ref, wukbd_ref, g_ref,
                 qlat_ref, c_ref, ki_ref, qit_ref, wit_ref):
    xb = x_ref[...].astype(BF16)
    main = jnp.dot(xb, w1_ref[...], preferred_element_type=F32)
    ckv = main[:, D_MODEL:D_MODEL + KV_LATENT]
    ms = jnp.mean(ckv * ckv, axis=-1, keepdims=True)
    c_ref[0] = (ckv * lax.rsqrt(ms + RMS_EPS) * g_ref[...]).astype(BF16)
    ki_ref[0] = main[:, D_MODEL + KV_LATENT:].astype(BF16)
    qscale = (HEAD_DIM ** -0.5) * LOG2E
    for g4 in range(N_HEAD_GROUPS):
        qg = main[:, g4 * 256:(g4 + 1) * 256].astype(BF16)
        ql = jnp.dot(qg, wukbd_ref[g4], preferred_element_type=F32) * qscale
        for hh in range(HEAD_GROUP):
            qlat_ref[0, g4 * HEAD_GROUP + hh] = ql[:, hh * KV_LATENT:(hh + 1) * KV_LATENT].astype(BF16)
    qit_ref[...] = (_nt_dot(wqit_ref[...], xb) * (IDX_DIM ** -0.5)).astype(BF16)
    wit_ref[...] = _nt_dot(wwit_ref[...], xb) * (IDX_HEADS ** -0.5)


def _proj_call(x2, w1, wqit, wwit, wukbd, g, bsz, seq, tm):
    t = bsz * seq
    per = seq // tm
    return pl.pallas_call(
        _proj_kernel,
        grid=(t // tm,),
        in_specs=[
            pl.BlockSpec((tm, D_MODEL), lambda i: (i, 0)),
            pl.BlockSpec(w1.shape, lambda i: (0, 0)),
            pl.BlockSpec(wqit.shape, lambda i: (0, 0)),
            pl.BlockSpec(wwit.shape, lambda i: (0, 0)),
            pl.BlockSpec(wukbd.shape, lambda i: (0, 0, 0)),
            pl.BlockSpec(g.shape, lambda i: (0, 0)),
        ],
        out_specs=[
            pl.BlockSpec((1, N_HEADS, tm, KV_LATENT), lambda i: (i // per, 0, i % per, 0)),
            pl.BlockSpec((1, tm, KV_LATENT), lambda i: (i // per, i % per, 0)),
            pl.BlockSpec((1, tm, IDX_DIM), lambda i: (i // per, i % per, 0)),
            pl.BlockSpec((IDX_HEADS * IDX_DIM, tm), lambda i: (0, i)),
            pl.BlockSpec((IDX_HEADS, tm), lambda i: (0, i)),
        ],
        out_shape=[
            jax.ShapeDtypeStruct((bsz, N_HEADS, seq, KV_LATENT), BF16),
            jax.ShapeDtypeStruct((bsz, seq, KV_LATENT), BF16),
            jax.ShapeDtypeStruct((bsz, seq, IDX_DIM), BF16),
            jax.ShapeDtypeStruct((IDX_HEADS * IDX_DIM, t), BF16),
            jax.ShapeDtypeStruct((IDX_HEADS, t), F32),
        ],
        compiler_params=pltpu.CompilerParams(
            dimension_semantics=("arbitrary",), vmem_limit_bytes=VMEM_LIMIT),
        name="dsa_proj",
    )(x2, w1, wqit, wwit, wukbd, g)


def _attn_kernel(qlat_ref, c_ref, ki_ref, qit_ref, wit_ref, o_ref,
                 st_ref, dm_ref, lg_ref, acc_ref, mx_ref, sm_ref):
    j = pl.program_id(1)
    nkc = (j + 2) // 2
    npair = (nkc + 1) // 2
    q0 = j * Q_TILE
    neg_inf = jnp.float32(-jnp.inf)
    pos_inf = jnp.float32(jnp.inf)
    vshape = (8, Q_TILE)

    qi_all = jnp.concatenate(
        [qit_ref[h * IDX_DIM:(h + 1) * IDX_DIM, :] for h in range(IDX_HEADS)], axis=1)
    wit = wit_ref[...]
    pair = 2 * K_CHUNK
    qpos2 = q0 + lax.broadcasted_iota(jnp.int32, (pair, Q_TILE), 1)
    krow2 = lax.broadcasted_iota(jnp.int32, (pair, Q_TILE), 0)
    qpos = q0 + lax.broadcasted_iota(jnp.int32, (K_CHUNK, Q_TILE), 1)
    krow = lax.broadcasted_iota(jnp.int32, (K_CHUNK, Q_TILE), 0)

    def score_body(kp, carry):
        mn, mxv = carry
        k0 = pl.multiple_of(kp * pair, pair)
        s_all = jnp.dot(ki_ref[0, pl.ds(k0, pair), :], qi_all,
                        preferred_element_type=F32)
        s = wit[0:1, :] * jnp.maximum(s_all[:, 0:Q_TILE], 0.0)
        for h in range(1, IDX_HEADS):
            s = s + wit[h:h + 1, :] * jnp.maximum(s_all[:, h * Q_TILE:(h + 1) * Q_TILE], 0.0)
        valid = (k0 + krow2) <= qpos2
        st_ref[pl.ds(k0, pair), :] = jnp.where(valid, s, neg_inf)
        mn = jnp.minimum(mn, _fold_rows(jnp.where(valid, s, pos_inf), jnp.min))
        mxv = jnp.maximum(mxv, _fold_rows(jnp.where(valid, s, neg_inf), jnp.max))
        return mn, mxv

    mn, mxv = lax.fori_loop(
        0, npair, score_body, (jnp.full(vshape, pos_inf, F32), jnp.full(vshape, neg_inf, F32)))
    rowmin = _lane_rep(mn, jnp.min)
    rowmax = _lane_rep(mxv, jnp.max)

    topk_f = jnp.float32(TOPK)

    def pair_block(kp):
        k0 = pl.multiple_of(kp * pair, pair)
        return st_ref[pl.ds(k0, pair), :]

    def count_ge(thr):
        def body(kp, acc):
            return acc + _fold_rows(jnp.where(pair_block(kp) >= thr[0:1, :], 1.0, 0.0), jnp.sum)
        return _lane_rep(lax.fori_loop(0, npair, body, jnp.zeros(vshape, F32)), jnp.sum)

    def count_snap(thr):
        def body(kp, carry):
            cnt, amin, bmx = carry
            blk = pair_block(kp)
            ge = blk >= thr[0:1, :]
            cnt = cnt + _fold_rows(jnp.where(ge, 1.0, 0.0), jnp.sum)
            amin = jnp.minimum(amin, _fold_rows(jnp.where(ge, blk, pos_inf), jnp.min))
            bmx = jnp.maximum(bmx, _fold_rows(jnp.where(ge, neg_inf, blk), jnp.max))
            return cnt, amin, bmx
        cnt, amin, bmx = lax.fori_loop(
            0, npair, body,
            (jnp.zeros(vshape, F32), jnp.full(vshape, pos_inf, F32), jnp.full(vshape, neg_inf, F32)))
        return _lane_rep(cnt, jnp.sum), _lane_rep(amin, jnp.min), _lane_rep(bmx, jnp.max)

    def unsettled(clo, bmin, bmax):
        return jnp.logical_and(clo > topk_f, bmin < bmax)

    def probe_point(lo, hi, bmin, bmax):
        a = jnp.maximum(lo, bmin)
        b = jnp.minimum(hi, bmax)
        mid = a * 0.5 + b * 0.5
        return jnp.where(mid <= lo, b, mid)

    def plain_body(i, state):
        lo, hi, clo, chi, bmin, bmax = state
        active = unsettled(clo, bmin, bmax)
        mid = jnp.where(i == 0, rowmax, probe_point(lo, hi, bmin, bmax))
        c = count_ge(mid)
        up = jnp.logical_and(active, c >= topk_f)
        dn = jnp.logical_and(active, c < topk_f)
        return (jnp.where(up, mid, lo), jnp.where(dn, mid, hi),
                jnp.where(up, c, clo), jnp.where(dn, c, chi), bmin, bmax)

    n_valid = (q0 + lax.broadcasted_iota(jnp.int32, vshape, 1) + 1).astype(F32)
    state = (rowmin, jnp.full(vshape, pos_inf, F32), n_valid, jnp.zeros(vshape, F32), rowmin, rowmax)
    state = lax.fori_loop(0, N_PLAIN_SEARCH, plain_body, state)

    def snap_flag(clo, bmin, bmax):
        return jnp.max(jnp.where(unsettled(clo, bmin, bmax), 1.0, 0.0))

    def snap_cond(carry):
        it, flag = carry[0], carry[1]
        return jnp.logical_and(flag > 0.0, it < MAX_SNAP_ITERS)

    def snap_body(carry):
        it, _, lo, hi, clo, chi, bmin, bmax = carry
        active = unsettled(clo, bmin, bmax)
        mid = probe_point(lo, hi, bmin, bmax)
        c, amin, bmx = count_snap(mid)
        up = jnp.logical_and(active, c >= topk_f)
        dn = jnp.logical_and(active, c < topk_f)
        lo, clo, bmin = jnp.where(up, mid, lo), jnp.where(up, c, clo), jnp.where(up, amin, bmin)
        hi, chi, bmax = jnp.where(dn, mid, hi), jnp.where(dn, c, chi), jnp.where(dn, bmx, bmax)
        return it + 1, snap_flag(clo, bmin, bmax), lo, hi, clo, chi, bmin, bmax

    lo, hi, clo, chi, bmin, bmax = state
    carry = lax.while_loop(snap_cond, snap_body,
                           (jnp.int32(0), snap_flag(clo, bmin, bmax), lo, hi, clo, chi, bmin, bmax))
    _, _, lo, hi, clo, chi, bmin, _ = carry

    tie_r = clo[0:1, :] > topk_f
    thr_r = jnp.where(tie_r, hi[0:1, :], lo[0:1, :])
    tv_r = bmin[0:1, :]
    need_r = topk_f - chi[0:1, :]
    tri = (lax.broadcasted_iota(jnp.int32, (K_CHUNK, K_CHUNK), 1)
           < lax.broadcasted_iota(jnp.int32, (K_CHUNK, K_CHUNK), 0)).astype(BF16)

    def mask_chunk(kc, run):
        k0 = pl.multiple_of(kc * K_CHUNK, K_CHUNK)
        blk = st_ref[pl.ds(k0, K_CHUNK), :]
        sel = blk >= thr_r
        eq = jnp.logical_and(tie_r, blk == tv_r)
        eq_f = jnp.where(eq, 1.0, 0.0)
        rank = jnp.dot(tri, eq_f.astype(BF16), preferred_element_type=F32) + run
        sel = jnp.logical_or(sel, jnp.logical_and(eq, rank < need_r))
        dist = ((k0 + krow) - qpos).astype(F32)
        dm_ref[kc] = jnp.where(sel, dist, MASK_BIAS).T
        return run + jnp.sum(eq_f, axis=0, keepdims=True)

    def mask_body(kp, run):
        return mask_chunk(2 * kp + 1, mask_chunk(2 * kp, run))

    lax.fori_loop(0, npair, mask_body, jnp.zeros((1, Q_TILE), F32))

    ql = qlat_ref[0].reshape(ATT_ROWS, KV_LATENT)
    slopes = [LOG2E * 2.0 ** (-8.0 * (h + 1) / N_HEADS) for h in range(N_HEADS)]
    mx_ref[...] = jnp.full(mx_ref.shape, neg_inf, F32)

    def fold_lanes(x, op):
        parts = [x[:, t * 128:(t + 1) * 128] for t in range(x.shape[1] // 128)]
        while len(parts) > 1:
            parts = [op(parts[2 * t], parts[2 * t + 1]) for t in range(len(parts) // 2)]
        return parts[0]

    def logit_block(kc0, nch):
        k0 = pl.multiple_of(kc0 * K_CHUNK, K_CHUNK)
        lg = _nt_dot(ql, c_ref[0, pl.ds(k0, nch * K_CHUNK), :])
        dmk = jnp.concatenate([dm_ref[kc0 + t] for t in range(nch)], axis=1)
        for h in range(N_HEADS):
            rs = slice(h * Q_TILE, (h + 1) * Q_TILE)
            lgh = lg[rs] + slopes[h] * dmk
            for t in range(nch):
                lg_ref[kc0 + t, rs, :] = lgh[:, t * K_CHUNK:(t + 1) * K_CHUNK]
            mx_ref[rs, :] = jnp.maximum(mx_ref[rs, :], fold_lanes(lgh, jnp.maximum))

    def logit_body(kp, _):
        logit_block(2 * kp, 2)
        return 0

    lax.fori_loop(0, nkc // 2, logit_body, 0)

    @pl.when(nkc % 2 == 1)
    def _():
        logit_block(nkc - 1, 1)

    m = jnp.max(mx_ref[...], axis=-1, keepdims=True)

    acc_ref[...] = jnp.zeros(acc_ref.shape, F32)
    sm_ref[...] = jnp.zeros(sm_ref.shape, F32)

    def pv_block(kc0, nch):
        k0 = pl.multiple_of(kc0 * K_CHUNK, K_CHUNK)
        p = jnp.concatenate([jnp.exp2(lg_ref[kc0 + t] - m) for t in range(nch)], axis=1)
        sm_ref[...] += fold_lanes(p, jnp.add)
        acc_ref[...] += jnp.dot(p.astype(BF16), c_ref[0, pl.ds(k0, nch * K_CHUNK), :],
                                preferred_element_type=F32)

    def pv_body(kp, _):
        pv_block(2 * kp, 2)
        return 0

    lax.fori_loop(0, nkc // 2, pv_body, 0)

    @pl.when(nkc % 2 == 1)
    def _():
        pv_block(nkc - 1, 1)
    inv = 1.0 / jnp.sum(sm_ref[...], axis=-1, keepdims=True)
    for h in range(N_HEADS):
        rs = slice(h * Q_TILE, (h + 1) * Q_TILE)
        o_ref[0, :, h * KV_LATENT:(h + 1) * KV_LATENT] = (acc_ref[rs, :] * inv[rs]).astype(BF16)


def _attn_call(qlat, c, ki, qit, wit, bsz, seq):
    nq = seq // Q_TILE
    n_chunks = seq // K_CHUNK
    return pl.pallas_call(
        _attn_kernel,
        grid=(bsz, nq),
        in_specs=[
            pl.BlockSpec((1, N_HEADS, Q_TILE, KV_LATENT), lambda b, j: (b, 0, j, 0)),
            pl.BlockSpec((1, seq, KV_LATENT), lambda b, j: (b, 0, 0)),
            pl.BlockSpec((1, seq, IDX_DIM), lambda b, j: (b, 0, 0)),
            pl.BlockSpec((IDX_HEADS * IDX_DIM, Q_TILE), lambda b, j: (0, b * nq + j)),
            pl.BlockSpec((IDX_HEADS, Q_TILE), lambda b, j: (0, b * nq + j)),
        ],
        out_specs=pl.BlockSpec((1, Q_TILE, N_HEADS * KV_LATENT), lambda b, j: (b, j, 0)),
        out_shape=jax.ShapeDtypeStruct((bsz, seq, N_HEADS * KV_LATENT), BF16),
        scratch_shapes=[
            pltpu.VMEM((seq, Q_TILE), F32),
            pltpu.VMEM((n_chunks, Q_TILE, K_CHUNK), F32),
            pltpu.VMEM((n_chunks, ATT_ROWS, K_CHUNK), F32),
            pltpu.VMEM((ATT_ROWS, KV_LATENT), F32),
            pltpu.VMEM((ATT_ROWS, 128), F32),
            pltpu.VMEM((ATT_ROWS, 128), F32),
        ],
        compiler_params=pltpu.CompilerParams(
            dimension_semantics=("arbitrary", "arbitrary"), vmem_limit_bytes=VMEM_LIMIT),
        name="dsa_attn",
    )(qlat, c, ki, qit, wit)


def _pool_kernel(h_ref, halo_ref, win_ref, wgrp_ref, scale_ref, y_ref):
    i = pl.program_id(1)
    tm = h_ref.shape[1]
    u_main = jnp.dot(h_ref[0].astype(BF16), win_ref[...], preferred_element_type=F32)
    u_halo = jnp.dot(halo_ref[0].astype(BF16), win_ref[...], preferred_element_type=F32)
    u_halo = u_halo * jnp.where(i > 0, 1.0, 0.0)
    u = jnp.concatenate([u_halo, u_main], axis=0)
    pos = i * tm + lax.broadcasted_iota(jnp.int32, (tm, 1), 0)
    for g, window in enumerate(POOL_WINDOWS):
        ug = u[:, g * POOL_GROUP_DIM:(g + 1) * POOL_GROUP_DIM]
        s = ug
        shift = 1
        while shift < window:
            s = s + pltpu.roll(s, shift, axis=0)
            shift *= 2
        cnt = jnp.minimum(pos + 1, window).astype(F32)
        pooled = (s[POOL_HALO:] / cnt - ug[POOL_HALO:]).astype(BF16)
        yg = jnp.dot(pooled, wgrp_ref[g], preferred_element_type=F32)
        yg = yg * scale_ref[:, g * POOL_GROUP_DIM:(g + 1) * POOL_GROUP_DIM]
        y_ref[0, :, g * POOL_GROUP_DIM:(g + 1) * POOL_GROUP_DIM] = yg.astype(BF16)


def _pool_call(h3, win, wgrp, scale, tm):
    bsz, seq, _ = h3.shape
    halo_per = tm // POOL_HALO
    return pl.pallas_call(
        _pool_kernel,
        grid=(bsz, seq // tm),
        in_specs=[
            pl.BlockSpec((1, tm, D_MODEL), lambda b, i: (b, i, 0)),
            pl.BlockSpec((1, POOL_HALO, D_MODEL), lambda b, i: (b, jnp.maximum(i * halo_per - 1, 0), 0)),
            pl.BlockSpec(win.shape, lambda b, i: (0, 0)),
            pl.BlockSpec(wgrp.shape, lambda b, i: (0, 0, 0)),
            pl.BlockSpec(scale.shape, lambda b, i: (0, 0)),
        ],
        out_specs=pl.BlockSpec((1, tm, D_MODEL), lambda b, i: (b, i, 0)),
        out_shape=jax.ShapeDtypeStruct((bsz, seq, D_MODEL), BF16),
        compiler_params=pltpu.CompilerParams(
            dimension_semantics=("arbitrary", "arbitrary"), vmem_limit_bytes=VMEM_LIMIT),
        name="pool_mix",
    )(h3, h3, win, wgrp, scale)


def _post_kernel(*refs, with_uv):
    if with_uv:
        (h_ref, mi_ref, wuv_ref, wmo_ref, g1_ref, b1_ref, wg_ref, wu_ref, wd_ref, g2_ref, b2_ref,
         out_ref, h1_ref, hb_ref, acc_ref) = refs
        group = HEAD_GROUP * KV_LATENT
        mi = jnp.concatenate(
            [jnp.dot(mi_ref[:, g * group:(g + 1) * group], wuv_ref[g],
                     preferred_element_type=F32).astype(BF16) for g in range(N_HEAD_GROUPS)], axis=1)
    else:
        (h_ref, mi_ref, wmo_ref, g1_ref, b1_ref, wg_ref, wu_ref, wd_ref, g2_ref, b2_ref,
         out_ref, h1_ref, hb_ref, acc_ref) = refs
        mi = mi_ref[...]
    mix = jnp.dot(mi, wmo_ref[...], preferred_element_type=F32)
    h1 = _layer_norm(DEEPNORM_ALPHA * h_ref[...] + mix, g1_ref[...], b1_ref[...])
    h1_ref[...] = h1
    hb_ref[...] = h1.astype(BF16)
    acc_ref[...] = jnp.zeros(acc_ref.shape, F32)

    def ff_body(f, _):
        hb = hb_ref[...]
        gate = jnp.dot(hb, wg_ref[f], preferred_element_type=F32)
        up = jnp.dot(hb, wu_ref[f], preferred_element_type=F32)
        act = (gate * (1.0 / (1.0 + jnp.exp(-gate))) * up).astype(BF16)
        acc_ref[...] += jnp.dot(act, wd_ref[f], preferred_element_type=F32)
        return 0

    lax.fori_loop(0, N_FF_CHUNKS, ff_body, 0, unroll=True)
    out_ref[...] = _layer_norm(DEEPNORM_ALPHA * h1_ref[...] + acc_ref[...], g2_ref[...], b2_ref[...])


def _post_call(h2, mi2, wuv, wmo, g1, b1, wg3, wu3, wd3, g2, b2, tm):
    t = h2.shape[0]
    const2 = lambda i: (0, 0)
    const3 = lambda i: (0, 0, 0)
    resident = dict(pipeline_mode=pl.Buffered(1))
    with_uv = wuv is not None
    in_specs = [
        pl.BlockSpec((tm, D_MODEL), lambda i: (i, 0)),
        pl.BlockSpec((tm, mi2.shape[1]), lambda i: (i, 0)),
    ]
    operands = [h2, mi2]
    if with_uv:
        in_specs.append(pl.BlockSpec(wuv.shape, const3, **resident))
        operands.append(wuv)
    in_specs += [
        pl.BlockSpec(wmo.shape, const2, **resident),
        pl.BlockSpec(g1.shape, const2),
        pl.BlockSpec(b1.shape, const2),
        pl.BlockSpec(wg3.shape, const3, **resident),
        pl.BlockSpec(wu3.shape, const3, **resident),
        pl.BlockSpec(wd3.shape, const3, **resident),
        pl.BlockSpec(g2.shape, const2),
        pl.BlockSpec(b2.shape, const2),
    ]
    operands += [wmo, g1, b1, wg3, wu3, wd3, g2, b2]
    return pl.pallas_call(
        functools.partial(_post_kernel, with_uv=with_uv),
        grid=(t // tm,),
        in_specs=in_specs,
        out_specs=pl.BlockSpec((tm, D_MODEL), lambda i: (i, 0)),
        out_shape=jax.ShapeDtypeStruct((t, D_MODEL), F32),
        scratch_shapes=[
            pltpu.VMEM((tm, D_MODEL), F32),
            pltpu.VMEM((tm, D_MODEL), BF16),
            pltpu.VMEM((tm, D_MODEL), F32),
        ],
        compiler_params=pltpu.CompilerParams(
            dimension_semantics=("arbitrary",), vmem_limit_bytes=VMEM_LIMIT),
        name="post_ffn_uv" if with_uv else "post_ffn",
    )(*operands)


def _ffn_weights(w_gu, w_down):
    wg3 = w_gu[:, :D_FF].reshape(D_MODEL, N_FF_CHUNKS, FF_CHUNK).transpose(1, 0, 2).astype(BF16)
    wu3 = w_gu[:, D_FF:].reshape(D_MODEL, N_FF_CHUNKS, FF_CHUNK).transpose(1, 0, 2).astype(BF16)
    wd3 = w_down.reshape(N_FF_CHUNKS, FF_CHUNK, D_MODEL).astype(BF16)
    return wg3, wu3, wd3


def kernel(x, a_w_in, a_w_uk, a_w_uv, a_kv_norm_g, a_w_o, b_w_in, b_w_grp, b_scale, b_w_o,
           f_w_gu, f_w_down, ln_mix_g, ln_mix_b, ln_ffn_g, ln_ffn_b):
    bsz, seq, _ = x.shape
    t = bsz * seq
    x2 = x.reshape(t, D_MODEL)
    row = lambda v: v.reshape(1, -1)

    w_in = a_w_in[0]
    o_c = D_MODEL + KV_LATENT
    o_qi = o_c + IDX_HEADS * IDX_DIM
    o_ki = o_qi + IDX_DIM
    w1 = jnp.concatenate([w_in[:, :o_c], w_in[:, o_qi:o_ki]], axis=1).astype(BF16)
    wqit = w_in[:, o_c:o_qi].T.astype(BF16)
    wwit = w_in[:, o_ki:].T.astype(BF16)
    eye = jnp.eye(HEAD_GROUP, dtype=F32)
    ukt = jnp.swapaxes(a_w_uk[0], 1, 2).reshape(N_HEAD_GROUPS, HEAD_GROUP, HEAD_DIM, KV_LATENT)
    wukbd = jnp.einsum('ghdc,hk->ghdkc', ukt, eye).reshape(
        N_HEAD_GROUPS, HEAD_GROUP * HEAD_DIM, HEAD_GROUP * KV_LATENT).astype(BF16)
    uv = a_w_uv[0].reshape(N_HEAD_GROUPS, HEAD_GROUP, KV_LATENT, HEAD_DIM)
    wuvbd = jnp.einsum('ghcd,hk->ghckd', uv, eye).reshape(
        N_HEAD_GROUPS, HEAD_GROUP * KV_LATENT, HEAD_GROUP * HEAD_DIM).astype(BF16)

    qlat, c, ki, qit, wit = _proj_call(x2, w1, wqit, wwit, wukbd, row(a_kv_norm_g[0]), bsz, seq, 512)
    olat = _attn_call(qlat, c, ki, qit, wit, bsz, seq)
    wg3, wu3, wd3 = _ffn_weights(f_w_gu[0], f_w_down[0])
    h = _post_call(x2, olat.reshape(t, N_HEADS * KV_LATENT), wuvbd, a_w_o[0].astype(BF16),
                   row(ln_mix_g[0]), row(ln_mix_b[0]), wg3, wu3, wd3,
                   row(ln_ffn_g[0]), row(ln_ffn_b[0]), 512)

    y = _pool_call(h.reshape(bsz, seq, D_MODEL), b_w_in[0].astype(BF16), b_w_grp[0].astype(BF16),
                   row(b_scale[0]), 512)
    wg3, wu3, wd3 = _ffn_weights(f_w_gu[1], f_w_down[1])
    h = _post_call(h, y.reshape(t, D_MODEL), None, b_w_o[0].astype(BF16),
                   row(ln_mix_g[1]), row(ln_mix_b[1]), wg3, wu3, wd3,
                   row(ln_ffn_g[1]), row(ln_ffn_b[1]), 512)
    return h.reshape(bsz, seq, D_MODEL)
```

```python
import functools
import math

import jax
import jax.numpy as jnp
from jax import lax
from jax.experimental import pallas as pl
from jax.experimental.pallas import tpu as pltpu

D_MODEL = 1024
N_HEADS = 16
HEAD_DIM = 64
KV_LATENT = 256
IDX_HEADS = 8
IDX_DIM = 64
TOPK = 256
POOL_WINDOWS = (2, 4, 8, 16)
POOL_GROUP_DIM = 256
D_FF = 2816
DEPTH = 2
DEEPNORM_ALPHA = (2 * DEPTH) ** 0.25
LN_EPS = 1e-5
RMS_EPS = 1e-6
LOG2E = math.log2(math.e)

Q_TILE = 128
K_CHUNK = 256
HEAD_GROUP = 4
N_HEAD_GROUPS = N_HEADS // HEAD_GROUP
ATT_ROWS = N_HEADS * Q_TILE
FF_CHUNK = 256
N_FF_CHUNKS = D_FF // FF_CHUNK
POOL_HALO = 16
MASK_BIAS = -1e30
N_PLAIN_SEARCH = 12
N_FIXED_SNAP = 4
MAX_SNAP_ITERS = 4096

BF16 = jnp.bfloat16
F32 = jnp.float32
VMEM_LIMIT = 56 * 1024 * 1024


def _nt_dot(a, b):
    return lax.dot_general(a, b, (((1,), (1,)), ((), ())), preferred_element_type=F32)


def _layer_norm(z, g, b):
    mu = jnp.mean(z, axis=-1, keepdims=True)
    zc = z - mu
    var = jnp.mean(zc * zc, axis=-1, keepdims=True)
    return zc * lax.rsqrt(var + LN_EPS) * g + b


def _fold_rows(x, op, parts=4):
    x = x.reshape(parts, -1, 8, x.shape[-1])
    y = op(x, axis=1)
    return op(y, axis=0)


def _lane_rep(x, op):
    return jnp.broadcast_to(op(x, axis=0, keepdims=True), x.shape)


def _proj_kernel(x_ref, w1_ref, wwit_ref, wukbd_ref, g_ref,
                 qlat_ref, c_ref, ki_ref, qit_ref, wit_ref):
    xb = x_ref[...].astype(BF16)
    o_c = D_MODEL + KV_LATENT
    o_qi = o_c + IDX_HEADS * IDX_DIM
    main = jnp.dot(xb, w1_ref[...], preferred_element_type=F32)
    ckv = main[:, D_MODEL:o_c]
    ms = jnp.mean(ckv * ckv, axis=-1, keepdims=True)
    c_ref[0] = (ckv * lax.rsqrt(ms + RMS_EPS) * g_ref[...]).astype(BF16)
    ki_ref[0] = main[:, o_qi:o_qi + IDX_DIM].astype(BF16)
    qscale = (HEAD_DIM ** -0.5) * LOG2E
    for g4 in range(N_HEAD_GROUPS):
        qg = main[:, g4 * 256:(g4 + 1) * 256].astype(BF16)
        ql = jnp.dot(qg, wukbd_ref[g4], preferred_element_type=F32) * qscale
        for hh in range(HEAD_GROUP):
            qlat_ref[0, g4 * HEAD_GROUP + hh] = ql[:, hh * KV_LATENT:(hh + 1) * KV_LATENT].astype(BF16)
    qit_ref[...] = (main[:, o_c:o_qi] * (IDX_DIM ** -0.5)).T.astype(BF16)
    wit_ref[...] = _nt_dot(wwit_ref[...], xb) * (IDX_HEADS ** -0.5)


def _proj_call(x2, w1, wwit, wukbd, g, bsz, seq, tm):
    t = bsz * seq
    per = seq // tm
    return pl.pallas_call(
        _proj_kernel,
        grid=(t // tm,),
        in_specs=[
            pl.BlockSpec((tm, D_MODEL), lambda i: (i, 0)),
            pl.BlockSpec(w1.shape, lambda i: (0, 0)),
            pl.BlockSpec(wwit.shape, lambda i: (0, 0)),
            pl.BlockSpec(wukbd.shape, lambda i: (0, 0, 0)),
            pl.BlockSpec(g.shape, lambda i: (0, 0)),
        ],
        out_specs=[
            pl.BlockSpec((1, N_HEADS, tm, KV_LATENT), lambda i: (i // per, 0, i % per, 0)),
            pl.BlockSpec((1, tm, KV_LATENT), lambda i: (i // per, i % per, 0)),
            pl.BlockSpec((1, tm, IDX_DIM), lambda i: (i // per, i % per, 0)),
            pl.BlockSpec((IDX_HEADS * IDX_DIM, tm), lambda i: (0, i)),
            pl.BlockSpec((IDX_HEADS, tm), lambda i: (0, i)),
        ],
        out_shape=[
            jax.ShapeDtypeStruct((bsz, N_HEADS, seq, KV_LATENT), BF16),
            jax.ShapeDtypeStruct((bsz, seq, KV_LATENT), BF16),
            jax.ShapeDtypeStruct((bsz, seq, IDX_DIM), BF16),
            jax.ShapeDtypeStruct((IDX_HEADS * IDX_DIM, t), BF16),
            jax.ShapeDtypeStruct((IDX_HEADS, t), F32),
        ],
        compiler_params=pltpu.CompilerParams(
            dimension_semantics=("arbitrary",), vmem_limit_bytes=VMEM_LIMIT),
        name="dsa_proj",
    )(x2, w1, wwit, wukbd, g)


def _attn_kernel(qlat_ref, c_ref, ki_ref, qit_ref, wit_ref, o_ref,
                 st_ref, dm_ref, lg_ref, acc_ref, mx_ref, sm_ref):
    j = pl.program_id(1)
    nkc = (j + 2) // 2
    npair = (nkc + 1) // 2
    q0 = j * Q_TILE
    neg_inf = jnp.float32(-jnp.inf)
    pos_inf = jnp.float32(jnp.inf)
    vshape = (8, Q_TILE)

    qi_all = jnp.concatenate(
        [qit_ref[h * IDX_DIM:(h + 1) * IDX_DIM, :] for h in range(IDX_HEADS)], axis=1)
    wit = wit_ref[...]
    pair = 2 * K_CHUNK
    qpos2 = q0 + lax.broadcasted_iota(jnp.int32, (pair, Q_TILE), 1)
    krow2 = lax.broadcasted_iota(jnp.int32, (pair, Q_TILE), 0)
    qpos = q0 + lax.broadcasted_iota(jnp.int32, (K_CHUNK, Q_TILE), 1)
    krow = lax.broadcasted_iota(jnp.int32, (K_CHUNK, Q_TILE), 0)

    def score_body(kp, carry):
        mn, mxv = carry
        k0 = pl.multiple_of(kp * pair, pair)
        s_all = jnp.dot(ki_ref[0, pl.ds(k0, pair), :], qi_all,
                        preferred_element_type=F32)
        s = wit[0:1, :] * jnp.maximum(s_all[:, 0:Q_TILE], 0.0)
        for h in range(1, IDX_HEADS):
            s = s + wit[h:h + 1, :] * jnp.maximum(s_all[:, h * Q_TILE:(h + 1) * Q_TILE], 0.0)
        valid = (k0 + krow2) <= qpos2
        st_ref[pl.ds(k0, pair), :] = jnp.where(valid, s, neg_inf)
        mn = jnp.minimum(mn, _fold_rows(jnp.where(valid, s, pos_inf), jnp.min))
        mxv = jnp.maximum(mxv, _fold_rows(jnp.where(valid, s, neg_inf), jnp.max))
        return mn, mxv

    mn, mxv = lax.fori_loop(
        0, npair, score_body, (jnp.full(vshape, pos_inf, F32), jnp.full(vshape, neg_inf, F32)))
    rowmin = _lane_rep(mn, jnp.min)
    rowmax = _lane_rep(mxv, jnp.max)

    topk_f = jnp.float32(TOPK)

    def pair_block(kp):
        k0 = pl.multiple_of(kp * pair, pair)
        return st_ref[pl.ds(k0, pair), :]

    def count_ge(thr):
        def body(kp, acc):
            return acc + _fold_rows(jnp.where(pair_block(kp) >= thr[0:1, :], 1.0, 0.0), jnp.sum)
        return _lane_rep(lax.fori_loop(0, npair, body, jnp.zeros(vshape, F32)), jnp.sum)

    def count_snap(thr):
        def body(kp, carry):
            cnt, amin, bmx = carry
            blk = pair_block(kp)
            ge = blk >= thr[0:1, :]
            cnt = cnt + _fold_rows(jnp.where(ge, 1.0, 0.0), jnp.sum)
            amin = jnp.minimum(amin, _fold_rows(jnp.where(ge, blk, pos_inf), jnp.min))
            bmx = jnp.maximum(bmx, _fold_rows(jnp.where(ge, neg_inf, blk), jnp.max))
            return cnt, amin, bmx
        cnt, amin, bmx = lax.fori_loop(
            0, npair, body,
            (jnp.zeros(vshape, F32), jnp.full(vshape, pos_inf, F32), jnp.full(vshape, neg_inf, F32)))
        return _lane_rep(cnt, jnp.sum), _lane_rep(amin, jnp.min), _lane_rep(bmx, jnp.max)

    def unsettled(clo, bmin, bmax):
        return jnp.logical_and(clo > topk_f, bmin < bmax)

    def probe_point(lo, hi, bmin, bmax):
        a = jnp.maximum(lo, bmin)
        b = jnp.minimum(hi, bmax)
        mid = a * 0.5 + b * 0.5
        return jnp.where(mid <= lo, b, mid)

    def plain_body(i, state):
        lo, hi, clo, chi, bmin, bmax = state
        active = unsettled(clo, bmin, bmax)
        mid = jnp.where(i == 0, rowmax, probe_point(lo, hi, bmin, bmax))
        c = count_ge(mid)
        up = jnp.logical_and(active, c >= topk_f)
        dn = jnp.logical_and(active, c < topk_f)
        return (jnp.where(up, mid, lo), jnp.where(dn, mid, hi),
                jnp.where(up, c, clo), jnp.where(dn, c, chi), bmin, bmax)

    n_valid = (q0 + lax.broadcasted_iota(jnp.int32, vshape, 1) + 1).astype(F32)
    state = (rowmin, jnp.full(vshape, pos_inf, F32), n_valid, jnp.zeros(vshape, F32), rowmin, rowmax)
    state = lax.fori_loop(0, N_PLAIN_SEARCH, plain_body, state)

    def snap_flag(clo, bmin, bmax):
        return jnp.max(jnp.where(unsettled(clo, bmin, bmax), 1.0, 0.0))

    def snap_cond(carry):
        it, flag = carry[0], carry[1]
        return jnp.logical_and(flag > 0.0, it < MAX_SNAP_ITERS)

    def snap_step(i, state):
        lo, hi, clo, chi, bmin, bmax = state
        active = unsettled(clo, bmin, bmax)
        mid = probe_point(lo, hi, bmin, bmax)
        c, amin, bmx = count_snap(mid)
        up = jnp.logical_and(active, c >= topk_f)
        dn = jnp.logical_and(active, c < topk_f)
        return (jnp.where(up, mid, lo), jnp.where(dn, mid, hi), jnp.where(up, c, clo),
                jnp.where(dn, c, chi), jnp.where(up, amin, bmin), jnp.where(dn, bmx, bmax))

    def snap_body(carry):
        state = snap_step(0, carry[2:])
        return (carry[0] + 1, snap_flag(state[2], state[4], state[5])) + state

    state = lax.fori_loop(0, N_FIXED_SNAP, snap_step, state)
    carry = lax.while_loop(snap_cond, snap_body,
                           (jnp.int32(0), snap_flag(state[2], state[4], state[5])) + state)
    _, _, lo, hi, clo, chi, bmin, _ = carry

    tie_r = clo[0:1, :] > topk_f
    thr_r = jnp.where(tie_r, hi[0:1, :], lo[0:1, :])
    tv_r = bmin[0:1, :]
    need_r = topk_f - chi[0:1, :]
    tri = (lax.broadcasted_iota(jnp.int32, (K_CHUNK, K_CHUNK), 1)
           < lax.broadcasted_iota(jnp.int32, (K_CHUNK, K_CHUNK), 0)).astype(BF16)

    def mask_chunk(kc, run):
        k0 = pl.multiple_of(kc * K_CHUNK, K_CHUNK)
        blk = st_ref[pl.ds(k0, K_CHUNK), :]
        sel = blk >= thr_r
        eq = jnp.logical_and(tie_r, blk == tv_r)
        eq_f = jnp.where(eq, 1.0, 0.0)
        rank = jnp.dot(tri, eq_f.astype(BF16), preferred_element_type=F32) + run
        sel = jnp.logical_or(sel, jnp.logical_and(eq, rank < need_r))
        dist = ((k0 + krow) - qpos).astype(F32)
        dm_ref[kc] = jnp.where(sel, dist, MASK_BIAS).T
        return run + jnp.sum(eq_f, axis=0, keepdims=True)

    def mask_body(kp, run):
        return mask_chunk(2 * kp + 1, mask_chunk(2 * kp, run))

    lax.fori_loop(0, npair, mask_body, jnp.zeros((1, Q_TILE), F32))

    ql = qlat_ref[0].reshape(ATT_ROWS, KV_LATENT)
    slopes = [LOG2E * 2.0 ** (-8.0 * (h + 1) / N_HEADS) for h in range(N_HEADS)]
    mx_ref[...] = jnp.full(mx_ref.shape, neg_inf, F32)

    def fold_lanes(x, op):
        parts = [x[:, t * 128:(t + 1) * 128] for t in range(x.shape[1] // 128)]
        while len(parts) > 1:
            parts = [op(parts[2 * t], parts[2 * t + 1]) for t in range(len(parts) // 2)]
        return parts[0]

    def logit_block(kc0, nch):
        k0 = pl.multiple_of(kc0 * K_CHUNK, K_CHUNK)
        lg = _nt_dot(ql, c_ref[0, pl.ds(k0, nch * K_CHUNK), :])
        dmk = jnp.concatenate([dm_ref[kc0 + t] for t in range(nch)], axis=1)
        for h in range(N_HEADS):
            rs = slice(h * Q_TILE, (h + 1) * Q_TILE)
            lgh = lg[rs] + slopes[h] * dmk
            for t in range(nch):
                lg_ref[kc0 + t, rs, :] = lgh[:, t * K_CHUNK:(t + 1) * K_CHUNK]
            mx_ref[rs, :] = jnp.maximum(mx_ref[rs, :], fold_lanes(lgh, jnp.maximum))

    def logit_body(kp, _):
        logit_block(2 * kp, 2)
        return 0

    lax.fori_loop(0, nkc // 2, logit_body, 0)

    @pl.when(nkc % 2 == 1)
    def _():
        logit_block(nkc - 1, 1)

    m = jnp.max(mx_ref[...], axis=-1, keepdims=True)

    acc_ref[...] = jnp.zeros(acc_ref.shape, F32)
    sm_ref[...] = jnp.zeros(sm_ref.shape, F32)

    def pv_block(kc0, nch):
        k0 = pl.multiple_of(kc0 * K_CHUNK, K_CHUNK)
        p = jnp.concatenate([jnp.exp2(lg_ref[kc0 + t] - m) for t in range(nch)], axis=1)
        sm_ref[...] += fold_lanes(p, jnp.add)
        acc_ref[...] += jnp.dot(p.astype(BF16), c_ref[0, pl.ds(k0, nch * K_CHUNK), :],
                                preferred_element_type=F32)

    def pv_body(kp, _):
        pv_block(2 * kp, 2)
        return 0

    lax.fori_loop(0, nkc // 2, pv_body, 0)

    @pl.when(nkc % 2 == 1)
    def _():
        pv_block(nkc - 1, 1)
    inv = 1.0 / jnp.sum(sm_ref[...], axis=-1, keepdims=True)
    for h in range(N_HEADS):
        rs = slice(h * Q_TILE, (h + 1) * Q_TILE)
        o_ref[0, :, h * KV_LATENT:(h + 1) * KV_LATENT] = (acc_ref[rs, :] * inv[rs]).astype(BF16)


def _attn_call(qlat, c, ki, qit, wit, bsz, seq):
    nq = seq // Q_TILE
    n_chunks = seq // K_CHUNK
    return pl.pallas_call(
        _attn_kernel,
        grid=(bsz, nq),
        in_specs=[
            pl.BlockSpec((1, N_HEADS, Q_TILE, KV_LATENT), lambda b, j: (b, 0, j, 0)),
            pl.BlockSpec((1, seq, KV_LATENT), lambda b, j: (b, 0, 0)),
            pl.BlockSpec((1, seq, IDX_DIM), lambda b, j: (b, 0, 0)),
            pl.BlockSpec((IDX_HEADS * IDX_DIM, Q_TILE), lambda b, j: (0, b * nq + j)),
            pl.BlockSpec((IDX_HEADS, Q_TILE), lambda b, j: (0, b * nq + j)),
        ],
        out_specs=pl.BlockSpec((1, Q_TILE, N_HEADS * KV_LATENT), lambda b, j: (b, j, 0)),
        out_shape=jax.ShapeDtypeStruct((bsz, seq, N_HEADS * KV_LATENT), BF16),
        scratch_shapes=[
            pltpu.VMEM((seq, Q_TILE), F32),
            pltpu.VMEM((n_chunks, Q_TILE, K_CHUNK), F32),
            pltpu.VMEM((n_chunks, ATT_ROWS, K_CHUNK), F32),
            pltpu.VMEM((ATT_ROWS, KV_LATENT), F32),
            pltpu.VMEM((ATT_ROWS, 128), F32),
            pltpu.VMEM((ATT_ROWS, 128), F32),
        ],
        compiler_params=pltpu.CompilerParams(
            dimension_semantics=("arbitrary", "arbitrary"), vmem_limit_bytes=VMEM_LIMIT),
        name="dsa_attn",
    )(qlat, c, ki, qit, wit)


def _pool_kernel(h_ref, halo_ref, win_ref, wgrp_ref, scale_ref, y_ref):
    i = pl.program_id(1)
    tm = h_ref.shape[1]
    u_main = jnp.dot(h_ref[0].astype(BF16), win_ref[...], preferred_element_type=F32)
    u_halo = jnp.dot(halo_ref[0].astype(BF16), win_ref[...], preferred_element_type=F32)
    u_halo = u_halo * jnp.where(i > 0, 1.0, 0.0)
    u = jnp.concatenate([u_halo, u_main], axis=0)
    pos = i * tm + lax.broadcasted_iota(jnp.int32, (tm, 1), 0)
    for g, window in enumerate(POOL_WINDOWS):
        ug = u[:, g * POOL_GROUP_DIM:(g + 1) * POOL_GROUP_DIM]
        s = ug
        shift = 1
        while shift < window:
            s = s + pltpu.roll(s, shift, axis=0)
            shift *= 2
        cnt = jnp.minimum(pos + 1, window).astype(F32)
        pooled = (s[POOL_HALO:] / cnt - ug[POOL_HALO:]).astype(BF16)
        yg = jnp.dot(pooled, wgrp_ref[g], preferred_element_type=F32)
        yg = yg * scale_ref[:, g * POOL_GROUP_DIM:(g + 1) * POOL_GROUP_DIM]
        y_ref[0, :, g * POOL_GROUP_DIM:(g + 1) * POOL_GROUP_DIM] = yg.astype(BF16)


def _pool_call(h3, win, wgrp, scale, tm):
    bsz, seq, _ = h3.shape
    halo_per = tm // POOL_HALO
    return pl.pallas_call(
        _pool_kernel,
        grid=(bsz, seq // tm),
        in_specs=[
            pl.BlockSpec((1, tm, D_MODEL), lambda b, i: (b, i, 0)),
            pl.BlockSpec((1, POOL_HALO, D_MODEL), lambda b, i: (b, jnp.maximum(i * halo_per - 1, 0), 0)),
            pl.BlockSpec(win.shape, lambda b, i: (0, 0)),
            pl.BlockSpec(wgrp.shape, lambda b, i: (0, 0, 0)),
            pl.BlockSpec(scale.shape, lambda b, i: (0, 0)),
        ],
        out_specs=pl.BlockSpec((1, tm, D_MODEL), lambda b, i: (b, i, 0)),
        out_shape=jax.ShapeDtypeStruct((bsz, seq, D_MODEL), BF16),
        compiler_params=pltpu.CompilerParams(
            dimension_semantics=("arbitrary", "arbitrary"), vmem_limit_bytes=VMEM_LIMIT),
        name="pool_mix",
    )(h3, h3, win, wgrp, scale)


def _post_kernel(*refs, with_uv):
    if with_uv:
        (h_ref, mi_ref, wuv_ref, wmo_ref, g1_ref, b1_ref, wgu_ref, wd_ref, g2_ref, b2_ref,
         out_ref, h1_ref, hb_ref, acc_ref) = refs
        group = HEAD_GROUP * KV_LATENT
        mi = jnp.concatenate(
            [jnp.dot(mi_ref[:, g * group:(g + 1) * group], wuv_ref[g],
                     preferred_element_type=F32).astype(BF16) for g in range(N_HEAD_GROUPS)], axis=1)
    else:
        (h_ref, mi_ref, wmo_ref, g1_ref, b1_ref, wgu_ref, wd_ref, g2_ref, b2_ref,
         out_ref, h1_ref, hb_ref, acc_ref) = refs
        mi = mi_ref[...]
    mix = jnp.dot(mi, wmo_ref[...], preferred_element_type=F32)
    h1 = _layer_norm(DEEPNORM_ALPHA * h_ref[...] + mix, g1_ref[...], b1_ref[...])
    h1_ref[...] = h1
    hb_ref[...] = h1.astype(BF16)
    acc_ref[...] = jnp.zeros(acc_ref.shape, F32)

    for f in range(N_FF_CHUNKS):
        hb = hb_ref[...]
        cols = slice(f * FF_CHUNK, (f + 1) * FF_CHUNK)
        gate = jnp.dot(hb, wgu_ref[:, cols], preferred_element_type=F32)
        up = jnp.dot(hb, wgu_ref[:, D_FF + f * FF_CHUNK:D_FF + (f + 1) * FF_CHUNK],
                     preferred_element_type=F32)
        act = (gate * (1.0 / (1.0 + jnp.exp(-gate))) * up).astype(BF16)
        acc_ref[...] += jnp.dot(act, wd_ref[cols, :], preferred_element_type=F32)
    out_ref[...] = _layer_norm(DEEPNORM_ALPHA * h1_ref[...] + acc_ref[...], g2_ref[...], b2_ref[...])


def _post_call(h2, mi2, wuv, wmo, g1, b1, wgu, wd, g2, b2, tm):
    t = h2.shape[0]
    const2 = lambda i: (0, 0)
    const3 = lambda i: (0, 0, 0)
    resident = dict(pipeline_mode=pl.Buffered(1))
    with_uv = wuv is not None
    in_specs = [
        pl.BlockSpec((tm, D_MODEL), lambda i: (i, 0)),
        pl.BlockSpec((tm, mi2.shape[1]), lambda i: (i, 0)),
    ]
    operands = [h2, mi2]
    if with_uv:
        in_specs.append(pl.BlockSpec(wuv.shape, const3, **resident))
        operands.append(wuv)
    in_specs += [
        pl.BlockSpec(wmo.shape, const2, **resident),
        pl.BlockSpec(g1.shape, const2),
        pl.BlockSpec(b1.shape, const2),
        pl.BlockSpec(wgu.shape, const2, **resident),
        pl.BlockSpec(wd.shape, const2, **resident),
        pl.BlockSpec(g2.shape, const2),
        pl.BlockSpec(b2.shape, const2),
    ]
    operands += [wmo, g1, b1, wgu, wd, g2, b2]
    return pl.pallas_call(
        functools.partial(_post_kernel, with_uv=with_uv),
        grid=(t // tm,),
        in_specs=in_specs,
        out_specs=pl.BlockSpec((tm, D_MODEL), lambda i: (i, 0)),
        out_shape=jax.ShapeDtypeStruct((t, D_MODEL), F32),
        scratch_shapes=[
            pltpu.VMEM((tm, D_MODEL), F32),
            pltpu.VMEM((tm, D_MODEL), BF16),
            pltpu.VMEM((tm, D_MODEL), F32),
        ],
        compiler_params=pltpu.CompilerParams(
            dimension_semantics=("arbitrary",), vmem_limit_bytes=VMEM_LIMIT),
        name="post_ffn_uv" if with_uv else "post_ffn",
    )(*operands)


def kernel(x, a_w_in, a_w_uk, a_w_uv, a_kv_norm_g, a_w_o, b_w_in, b_w_grp, b_scale, b_w_o,
           f_w_gu, f_w_down, ln_mix_g, ln_mix_b, ln_ffn_g, ln_ffn_b):
    bsz, seq, _ = x.shape
    t = bsz * seq
    x2 = x.reshape(t, D_MODEL)
    row = lambda v: v.reshape(1, -1)

    w_in = a_w_in[0]
    o_c = D_MODEL + KV_LATENT
    o_qi = o_c + IDX_HEADS * IDX_DIM
    o_ki = o_qi + IDX_DIM
    w1 = w_in.astype(BF16)
    wwit = w_in[:, o_ki:].T.astype(BF16)
    eye = jnp.eye(HEAD_GROUP, dtype=F32)
    ukt = jnp.swapaxes(a_w_uk[0], 1, 2).reshape(N_HEAD_GROUPS, HEAD_GROUP, HEAD_DIM, KV_LATENT)
    wukbd = jnp.einsum('ghdc,hk->ghdkc', ukt, eye).reshape(
        N_HEAD_GROUPS, HEAD_GROUP * HEAD_DIM, HEAD_GROUP * KV_LATENT).astype(BF16)
    uv = a_w_uv[0].reshape(N_HEAD_GROUPS, HEAD_GROUP, KV_LATENT, HEAD_DIM)
    wuvbd = jnp.einsum('ghcd,hk->ghckd', uv, eye).reshape(
        N_HEAD_GROUPS, HEAD_GROUP * KV_LATENT, HEAD_GROUP * HEAD_DIM).astype(BF16)

    qlat, c, ki, qit, wit = _proj_call(x2, w1, wwit, wukbd, row(a_kv_norm_g[0]), bsz, seq, 512)
    olat = _attn_call(qlat, c, ki, qit, wit, bsz, seq)
    h = _post_call(x2, olat.reshape(t, N_HEADS * KV_LATENT), wuvbd, a_w_o[0].astype(BF16),
                   row(ln_mix_g[0]), row(ln_mix_b[0]), f_w_gu[0].astype(BF16), f_w_down[0].astype(BF16),
                   row(ln_ffn_g[0]), row(ln_ffn_b[0]), 512)

    y = _pool_call(h.reshape(bsz, seq, D_MODEL), b_w_in[0].astype(BF16), b_w_grp[0].astype(BF16),
                   row(b_scale[0]), 512)
    h = _post_call(h, y.reshape(t, D_MODEL), None, b_w_o[0].astype(BF16),
                   row(ln_mix_g[1]), row(ln_mix_b[1]), f_w_gu[1].astype(BF16), f_w_down[1].astype(BF16),
                   row(ln_ffn_g[1]), row(ln_ffn_b[1]), 512)
    return h.reshape(bsz, seq, D_MODEL)
```

```python
import functools
import math

import jax
import jax.numpy as jnp
from jax import lax
from jax.experimental import pallas as pl
from jax.experimental.pallas import tpu as pltpu

D_MODEL = 1024
N_HEADS = 16
HEAD_DIM = 64
KV_LATENT = 256
IDX_HEADS = 8
IDX_DIM = 64
TOPK = 256
POOL_WINDOWS = (2, 4, 8, 16)
POOL_GROUP_DIM = 256
D_FF = 2816
DEPTH = 2
DEEPNORM_ALPHA = (2 * DEPTH) ** 0.25
LN_EPS = 1e-5
RMS_EPS = 1e-6
LOG2E = math.log2(math.e)

Q_TILE = 256
K_CHUNK = 256
HEAD_GROUP = 4
N_HEAD_GROUPS = N_HEADS // HEAD_GROUP
ATT_HEADS = 8
ATT_ROWS = ATT_HEADS * Q_TILE
FF_CHUNK = 256
N_FF_CHUNKS = D_FF // FF_CHUNK
POST_ROWS = 512
POOL_HALO = 16
MASK_BIAS = -1e30
N_PLAIN_SEARCH = 12
N_FIXED_SNAP = 4
MAX_SNAP_ITERS = 4096

BF16 = jnp.bfloat16
F32 = jnp.float32
VMEM_LIMIT = 56 * 1024 * 1024


def _nt_dot(a, b):
    return lax.dot_general(a, b, (((1,), (1,)), ((), ())), preferred_element_type=F32)


def _layer_norm(z, g, b):
    mu = jnp.mean(z, axis=-1, keepdims=True)
    zc = z - mu
    var = jnp.mean(zc * zc, axis=-1, keepdims=True)
    return zc * lax.rsqrt(var + LN_EPS) * g + b


def _fold_rows(x, op, parts=4):
    x = x.reshape(parts, -1, 8, x.shape[-1])
    y = op(x, axis=1)
    return op(y, axis=0)


def _lane_rep(x, op):
    return jnp.broadcast_to(op(x, axis=0, keepdims=True), x.shape)


def _proj_kernel(x_ref, w1_ref, wwit_ref, wukbd_ref, g_ref,
                 qlat_ref, c_ref, ki_ref, qit_ref, wit_ref):
    xb = x_ref[...].astype(BF16)
    o_c = D_MODEL + KV_LATENT
    o_qi = o_c + IDX_HEADS * IDX_DIM
    main = jnp.dot(xb, w1_ref[...], preferred_element_type=F32)
    ckv = main[:, D_MODEL:o_c]
    ms = jnp.mean(ckv * ckv, axis=-1, keepdims=True)
    c_ref[0] = (ckv * lax.rsqrt(ms + RMS_EPS) * g_ref[...]).astype(BF16)
    ki_ref[0] = main[:, o_qi:o_qi + IDX_DIM].astype(BF16)
    qscale = (HEAD_DIM ** -0.5) * LOG2E
    for g4 in range(N_HEAD_GROUPS):
        qg = main[:, g4 * 256:(g4 + 1) * 256].astype(BF16)
        ql = jnp.dot(qg, wukbd_ref[g4], preferred_element_type=F32) * qscale
        for hh in range(HEAD_GROUP):
            qlat_ref[0, g4 * HEAD_GROUP + hh] = ql[:, hh * KV_LATENT:(hh + 1) * KV_LATENT].astype(BF16)
    qit_ref[...] = (main[:, o_c:o_qi] * (IDX_DIM ** -0.5)).T.astype(BF16)
    wit_ref[...] = _nt_dot(wwit_ref[...], xb) * (IDX_HEADS ** -0.5)


def _proj_call(x2, w1, wwit, wukbd, g, bsz, seq, tm):
    t = bsz * seq
    per = seq // tm
    return pl.pallas_call(
        _proj_kernel,
        grid=(t // tm,),
        in_specs=[
            pl.BlockSpec((tm, D_MODEL), lambda i: (i, 0)),
            pl.BlockSpec(w1.shape, lambda i: (0, 0)),
            pl.BlockSpec(wwit.shape, lambda i: (0, 0)),
            pl.BlockSpec(wukbd.shape, lambda i: (0, 0, 0)),
            pl.BlockSpec(g.shape, lambda i: (0, 0)),
        ],
        out_specs=[
            pl.BlockSpec((1, N_HEADS, tm, KV_LATENT), lambda i: (i // per, 0, i % per, 0)),
            pl.BlockSpec((1, tm, KV_LATENT), lambda i: (i // per, i % per, 0)),
            pl.BlockSpec((1, tm, IDX_DIM), lambda i: (i // per, i % per, 0)),
            pl.BlockSpec((IDX_HEADS * IDX_DIM, tm), lambda i: (0, i)),
            pl.BlockSpec((IDX_HEADS, tm), lambda i: (0, i)),
        ],
        out_shape=[
            jax.ShapeDtypeStruct((bsz, N_HEADS, seq, KV_LATENT), BF16),
            jax.ShapeDtypeStruct((bsz, seq, KV_LATENT), BF16),
            jax.ShapeDtypeStruct((bsz, seq, IDX_DIM), BF16),
            jax.ShapeDtypeStruct((IDX_HEADS * IDX_DIM, t), BF16),
            jax.ShapeDtypeStruct((IDX_HEADS, t), F32),
        ],
        compiler_params=pltpu.CompilerParams(
            dimension_semantics=("arbitrary",), vmem_limit_bytes=VMEM_LIMIT),
        name="dsa_proj",
    )(x2, w1, wwit, wukbd, g)


def _attn_kernel(qlat_ref, c_ref, ki_ref, qit_ref, wit_ref, o_ref,
                 st_ref, dm_ref, lg_ref, acc_ref, mx_ref, sm_ref):
    j = pl.program_id(1)
    nkc = j + 1
    q0 = j * Q_TILE
    neg_inf = jnp.float32(-jnp.inf)
    pos_inf = jnp.float32(jnp.inf)
    vshape = (8, Q_TILE)

    qi_all = jnp.concatenate(
        [qit_ref[h * IDX_DIM:(h + 1) * IDX_DIM, :] for h in range(IDX_HEADS)], axis=1)
    wit = wit_ref[...]
    qpos = q0 + lax.broadcasted_iota(jnp.int32, (K_CHUNK, Q_TILE), 1)
    krow = lax.broadcasted_iota(jnp.int32, (K_CHUNK, Q_TILE), 0)

    def score_body(kc, carry):
        mn, mxv = carry
        k0 = pl.multiple_of(kc * K_CHUNK, K_CHUNK)
        s_all = jnp.dot(ki_ref[0, pl.ds(k0, K_CHUNK), :], qi_all,
                        preferred_element_type=F32)
        s = wit[0:1, :] * jnp.maximum(s_all[:, 0:Q_TILE], 0.0)
        for h in range(1, IDX_HEADS):
            s = s + wit[h:h + 1, :] * jnp.maximum(s_all[:, h * Q_TILE:(h + 1) * Q_TILE], 0.0)
        valid = (k0 + krow) <= qpos
        st_ref[pl.ds(k0, K_CHUNK), :] = jnp.where(valid, s, neg_inf)
        mn = jnp.minimum(mn, _fold_rows(jnp.where(valid, s, pos_inf), jnp.min))
        mxv = jnp.maximum(mxv, _fold_rows(jnp.where(valid, s, neg_inf), jnp.max))
        return mn, mxv

    mn, mxv = lax.fori_loop(
        0, nkc, score_body, (jnp.full(vshape, pos_inf, F32), jnp.full(vshape, neg_inf, F32)))
    rowmin = _lane_rep(mn, jnp.min)
    rowmax = _lane_rep(mxv, jnp.max)

    topk_f = jnp.float32(TOPK)

    def score_block(kc):
        k0 = pl.multiple_of(kc * K_CHUNK, K_CHUNK)
        return st_ref[pl.ds(k0, K_CHUNK), :]

    def count_ge(thr):
        def body(kc, acc):
            return acc + _fold_rows(jnp.where(score_block(kc) >= thr[0:1, :], 1.0, 0.0), jnp.sum)
        return _lane_rep(lax.fori_loop(0, nkc, body, jnp.zeros(vshape, F32)), jnp.sum)

    def count_snap(thr):
        def body(kc, carry):
            cnt, amin, bmx = carry
            blk = score_block(kc)
            ge = blk >= thr[0:1, :]
            cnt = cnt + _fold_rows(jnp.where(ge, 1.0, 0.0), jnp.sum)
            amin = jnp.minimum(amin, _fold_rows(jnp.where(ge, blk, pos_inf), jnp.min))
            bmx = jnp.maximum(bmx, _fold_rows(jnp.where(ge, neg_inf, blk), jnp.max))
            return cnt, amin, bmx
        cnt, amin, bmx = lax.fori_loop(
            0, nkc, body,
            (jnp.zeros(vshape, F32), jnp.full(vshape, pos_inf, F32), jnp.full(vshape, neg_inf, F32)))
        return _lane_rep(cnt, jnp.sum), _lane_rep(amin, jnp.min), _lane_rep(bmx, jnp.max)

    def unsettled(clo, bmin, bmax):
        return jnp.logical_and(clo > topk_f, bmin < bmax)

    def probe_point(lo, hi, bmin, bmax):
        a = jnp.maximum(lo, bmin)
        b = jnp.minimum(hi, bmax)
        mid = a * 0.5 + b * 0.5
        return jnp.where(mid <= lo, b, mid)

    def plain_body(i, state):
        lo, hi, clo, chi, bmin, bmax = state
        active = unsettled(clo, bmin, bmax)
        mid = jnp.where(i == 0, rowmax, probe_point(lo, hi, bmin, bmax))
        c = count_ge(mid)
        up = jnp.logical_and(active, c >= topk_f)
        dn = jnp.logical_and(active, c < topk_f)
        return (jnp.where(up, mid, lo), jnp.where(dn, mid, hi),
                jnp.where(up, c, clo), jnp.where(dn, c, chi), bmin, bmax)

    n_valid = (q0 + lax.broadcasted_iota(jnp.int32, vshape, 1) + 1).astype(F32)
    state = (rowmin, jnp.full(vshape, pos_inf, F32), n_valid, jnp.zeros(vshape, F32), rowmin, rowmax)
    state = lax.fori_loop(0, N_PLAIN_SEARCH, plain_body, state)

    def snap_flag(clo, bmin, bmax):
        return jnp.max(jnp.where(unsettled(clo, bmin, bmax), 1.0, 0.0))

    def snap_cond(carry):
        it, flag = carry[0], carry[1]
        return jnp.logical_and(flag > 0.0, it < MAX_SNAP_ITERS)

    def snap_step(i, state):
        lo, hi, clo, chi, bmin, bmax = state
        active = unsettled(clo, bmin, bmax)
        mid = probe_point(lo, hi, bmin, bmax)
        c, amin, bmx = count_snap(mid)
        up = jnp.logical_and(active, c >= topk_f)
        dn = jnp.logical_and(active, c < topk_f)
        return (jnp.where(up, mid, lo), jnp.where(dn, mid, hi), jnp.where(up, c, clo),
                jnp.where(dn, c, chi), jnp.where(up, amin, bmin), jnp.where(dn, bmx, bmax))

    def snap_body(carry):
        state = snap_step(0, carry[2:])
        return (carry[0] + 1, snap_flag(state[2], state[4], state[5])) + state

    state = lax.fori_loop(0, N_FIXED_SNAP, snap_step, state)
    carry = lax.while_loop(snap_cond, snap_body,
                           (jnp.int32(0), snap_flag(state[2], state[4], state[5])) + state)
    _, _, lo, hi, clo, chi, bmin, _ = carry

    tie_r = clo[0:1, :] > topk_f
    thr_r = jnp.where(tie_r, hi[0:1, :], lo[0:1, :])
    tv_r = bmin[0:1, :]
    need_r = topk_f - chi[0:1, :]
    tri = (lax.broadcasted_iota(jnp.int32, (K_CHUNK, K_CHUNK), 1)
           < lax.broadcasted_iota(jnp.int32, (K_CHUNK, K_CHUNK), 0)).astype(BF16)

    def mask_chunk(kc, run):
        k0 = pl.multiple_of(kc * K_CHUNK, K_CHUNK)
        blk = st_ref[pl.ds(k0, K_CHUNK), :]
        sel = blk >= thr_r
        eq = jnp.logical_and(tie_r, blk == tv_r)
        eq_f = jnp.where(eq, 1.0, 0.0)
        rank = jnp.dot(tri, eq_f.astype(BF16), preferred_element_type=F32) + run
        sel = jnp.logical_or(sel, jnp.logical_and(eq, rank < need_r))
        dist = ((k0 + krow) - qpos).astype(F32)
        dm_ref[kc] = jnp.where(sel, dist, MASK_BIAS).T
        return run + jnp.sum(eq_f, axis=0, keepdims=True)

    lax.fori_loop(0, nkc, mask_chunk, jnp.zeros((1, Q_TILE), F32))

    def fold_lanes(x, op):
        parts = [x[:, t * 128:(t + 1) * 128] for t in range(x.shape[1] // 128)]
        while len(parts) > 1:
            parts = [op(parts[2 * t], parts[2 * t + 1]) for t in range(len(parts) // 2)]
        return parts[0]

    for g in range(N_HEADS // ATT_HEADS):
        heads = range(g * ATT_HEADS, (g + 1) * ATT_HEADS)
        ql = qlat_ref[0, g * ATT_HEADS:(g + 1) * ATT_HEADS].reshape(ATT_ROWS, KV_LATENT)
        slopes = [LOG2E * 2.0 ** (-8.0 * (h + 1) / N_HEADS) for h in heads]
        mx_ref[...] = jnp.full(mx_ref.shape, neg_inf, F32)

        def logit_block(kc0, nch, ql=ql, slopes=slopes):
            k0 = pl.multiple_of(kc0 * K_CHUNK, K_CHUNK)
            lg = _nt_dot(ql, c_ref[0, pl.ds(k0, nch * K_CHUNK), :])
            dmk = jnp.concatenate([dm_ref[kc0 + t] for t in range(nch)], axis=1)
            for hh in range(ATT_HEADS):
                rs = slice(hh * Q_TILE, (hh + 1) * Q_TILE)
                lgh = lg[rs] + slopes[hh] * dmk
                for t in range(nch):
                    lg_ref[kc0 + t, rs, :] = lgh[:, t * K_CHUNK:(t + 1) * K_CHUNK]
                mx_ref[rs, :] = jnp.maximum(mx_ref[rs, :], fold_lanes(lgh, jnp.maximum))

        def logit_body(kp, _, logit_block=logit_block):
            logit_block(2 * kp, 2)
            return 0

        lax.fori_loop(0, nkc // 2, logit_body, 0)

        @pl.when(nkc % 2 == 1)
        def _(logit_block=logit_block):
            logit_block(nkc - 1, 1)

        m = jnp.max(mx_ref[...], axis=-1, keepdims=True)

        acc_ref[...] = jnp.zeros(acc_ref.shape, F32)
        sm_ref[...] = jnp.zeros(sm_ref.shape, F32)

        def pv_block(kc0, nch, m=m):
            k0 = pl.multiple_of(kc0 * K_CHUNK, K_CHUNK)
            p = jnp.concatenate([jnp.exp2(lg_ref[kc0 + t] - m) for t in range(nch)], axis=1)
            sm_ref[...] += fold_lanes(p, jnp.add)
            acc_ref[...] += jnp.dot(p.astype(BF16), c_ref[0, pl.ds(k0, nch * K_CHUNK), :],
                                    preferred_element_type=F32)

        def pv_body(kp, _, pv_block=pv_block):
            pv_block(2 * kp, 2)
            return 0

        lax.fori_loop(0, nkc // 2, pv_body, 0)

        @pl.when(nkc % 2 == 1)
        def _(pv_block=pv_block):
            pv_block(nkc - 1, 1)

        inv = 1.0 / jnp.sum(sm_ref[...], axis=-1, keepdims=True)
        for hh, h in enumerate(heads):
            rs = slice(hh * Q_TILE, (hh + 1) * Q_TILE)
            o_ref[0, :, h * KV_LATENT:(h + 1) * KV_LATENT] = (acc_ref[rs, :] * inv[rs]).astype(BF16)


def _attn_call(qlat, c, ki, qit, wit, bsz, seq):
    nq = seq // Q_TILE
    n_chunks = seq // K_CHUNK
    return pl.pallas_call(
        _attn_kernel,
        grid=(bsz, nq),
        in_specs=[
            pl.BlockSpec((1, N_HEADS, Q_TILE, KV_LATENT), lambda b, j: (b, 0, j, 0)),
            pl.BlockSpec((1, seq, KV_LATENT), lambda b, j: (b, 0, 0)),
            pl.BlockSpec((1, seq, IDX_DIM), lambda b, j: (b, 0, 0)),
            pl.BlockSpec((IDX_HEADS * IDX_DIM, Q_TILE), lambda b, j: (0, b * nq + j)),
            pl.BlockSpec((IDX_HEADS, Q_TILE), lambda b, j: (0, b * nq + j)),
        ],
        out_specs=pl.BlockSpec((1, Q_TILE, N_HEADS * KV_LATENT), lambda b, j: (b, j, 0)),
        out_shape=jax.ShapeDtypeStruct((bsz, seq, N_HEADS * KV_LATENT), BF16),
        scratch_shapes=[
            pltpu.VMEM((seq, Q_TILE), F32),
            pltpu.VMEM((n_chunks, Q_TILE, K_CHUNK), F32),
            pltpu.VMEM((n_chunks, ATT_ROWS, K_CHUNK), F32),
            pltpu.VMEM((ATT_ROWS, KV_LATENT), F32),
            pltpu.VMEM((ATT_ROWS, 128), F32),
            pltpu.VMEM((ATT_ROWS, 128), F32),
        ],
        compiler_params=pltpu.CompilerParams(
            dimension_semantics=("arbitrary", "arbitrary"), vmem_limit_bytes=VMEM_LIMIT),
        name="dsa_attn",
    )(qlat, c, ki, qit, wit)


def _pool_kernel(h_ref, halo_ref, win_ref, wgrp_ref, scale_ref, y_ref):
    i = pl.program_id(1)
    tm = h_ref.shape[1]
    u_main = jnp.dot(h_ref[0].astype(BF16), win_ref[...], preferred_element_type=F32)
    u_halo = jnp.dot(halo_ref[0].astype(BF16), win_ref[...], preferred_element_type=F32)
    u_halo = u_halo * jnp.where(i > 0, 1.0, 0.0)
    u = jnp.concatenate([u_halo, u_main], axis=0)
    pos = i * tm + lax.broadcasted_iota(jnp.int32, (tm, 1), 0)
    for g, window in enumerate(POOL_WINDOWS):
        ug = u[:, g * POOL_GROUP_DIM:(g + 1) * POOL_GROUP_DIM]
        s = ug
        shift = 1
        while shift < window:
            s = s + pltpu.roll(s, shift, axis=0)
            shift *= 2
        cnt = jnp.minimum(pos + 1, window).astype(F32)
        pooled = (s[POOL_HALO:] / cnt - ug[POOL_HALO:]).astype(BF16)
        yg = jnp.dot(pooled, wgrp_ref[g], preferred_element_type=F32)
        yg = yg * scale_ref[:, g * POOL_GROUP_DIM:(g + 1) * POOL_GROUP_DIM]
        y_ref[0, :, g * POOL_GROUP_DIM:(g + 1) * POOL_GROUP_DIM] = yg.astype(BF16)


def _pool_call(h3, win, wgrp, scale, tm):
    bsz, seq, _ = h3.shape
    halo_per = tm // POOL_HALO
    return pl.pallas_call(
        _pool_kernel,
        grid=(bsz, seq // tm),
        in_specs=[
            pl.BlockSpec((1, tm, D_MODEL), lambda b, i: (b, i, 0)),
            pl.BlockSpec((1, POOL_HALO, D_MODEL), lambda b, i: (b, jnp.maximum(i * halo_per - 1, 0), 0)),
            pl.BlockSpec(win.shape, lambda b, i: (0, 0)),
            pl.BlockSpec(wgrp.shape, lambda b, i: (0, 0, 0)),
            pl.BlockSpec(scale.shape, lambda b, i: (0, 0)),
        ],
        out_specs=pl.BlockSpec((1, tm, D_MODEL), lambda b, i: (b, i, 0)),
        out_shape=jax.ShapeDtypeStruct((bsz, seq, D_MODEL), BF16),
        compiler_params=pltpu.CompilerParams(
            dimension_semantics=("arbitrary", "arbitrary"), vmem_limit_bytes=VMEM_LIMIT),
        name="pool_mix",
    )(h3, h3, win, wgrp, scale)


def _post_kernel(*refs, with_uv):
    if with_uv:
        (h_ref, mi_ref, wuv_ref, wmo_ref, g1_ref, b1_ref, wgu_ref, wd_ref, g2_ref, b2_ref,
         out_ref, h1_ref, hb_ref, acc_ref) = refs
    else:
        (h_ref, mi_ref, wmo_ref, g1_ref, b1_ref, wgu_ref, wd_ref, g2_ref, b2_ref,
         out_ref, h1_ref, hb_ref, acc_ref) = refs
    rows = POST_ROWS
    group = HEAD_GROUP * KV_LATENT

    for rs in [slice(r * rows, (r + 1) * rows) for r in range(h_ref.shape[0] // rows)]:
        if with_uv:
            mi = jnp.concatenate(
                [jnp.dot(mi_ref[rs, g * group:(g + 1) * group], wuv_ref[g],
                         preferred_element_type=F32).astype(BF16) for g in range(N_HEAD_GROUPS)], axis=1)
        else:
            mi = mi_ref[rs, :]
        mix = jnp.dot(mi, wmo_ref[...], preferred_element_type=F32)
        h1 = _layer_norm(DEEPNORM_ALPHA * h_ref[rs, :] + mix, g1_ref[...], b1_ref[...])
        h1_ref[rs, :] = h1
        hb_ref[rs, :] = h1.astype(BF16)
        acc_ref[rs, :] = jnp.zeros((rows, D_MODEL), F32)

        for f in range(N_FF_CHUNKS):
            hb = hb_ref[rs, :]
            cols = slice(f * FF_CHUNK, (f + 1) * FF_CHUNK)
            gate = jnp.dot(hb, wgu_ref[:, cols], preferred_element_type=F32)
            up = jnp.dot(hb, wgu_ref[:, D_FF + f * FF_CHUNK:D_FF + (f + 1) * FF_CHUNK],
                         preferred_element_type=F32)
            act = (gate * (1.0 / (1.0 + jnp.exp(-gate))) * up).astype(BF16)
            acc_ref[rs, :] += jnp.dot(act, wd_ref[cols, :], preferred_element_type=F32)
        out_ref[rs, :] = _layer_norm(DEEPNORM_ALPHA * h1_ref[rs, :] + acc_ref[rs, :],
                                     g2_ref[...], b2_ref[...])


def _post_call(h2, mi2, wuv, wmo, g1, b1, wgu, wd, g2, b2, tm):
    t = h2.shape[0]
    const2 = lambda i: (0, 0)
    const3 = lambda i: (0, 0, 0)
    resident = dict(pipeline_mode=pl.Buffered(1))
    with_uv = wuv is not None
    in_specs = [
        pl.BlockSpec((tm, D_MODEL), lambda i: (i, 0)),
        pl.BlockSpec((tm, mi2.shape[1]), lambda i: (i, 0)),
    ]
    operands = [h2, mi2]
    if with_uv:
        in_specs.append(pl.BlockSpec(wuv.shape, const3, **resident))
        operands.append(wuv)
    in_specs += [
        pl.BlockSpec(wmo.shape, const2, **resident),
        pl.BlockSpec(g1.shape, const2),
        pl.BlockSpec(b1.shape, const2),
        pl.BlockSpec(wgu.shape, const2, **resident),
        pl.BlockSpec(wd.shape, const2, **resident),
        pl.BlockSpec(g2.shape, const2),
        pl.BlockSpec(b2.shape, const2),
    ]
    operands += [wmo, g1, b1, wgu, wd, g2, b2]
    return pl.pallas_call(
        functools.partial(_post_kernel, with_uv=with_uv),
        grid=(t // tm,),
        in_specs=in_specs,
        out_specs=pl.BlockSpec((tm, D_MODEL), lambda i: (i, 0)),
        out_shape=jax.ShapeDtypeStruct((t, D_MODEL), F32),
        scratch_shapes=[
            pltpu.VMEM((tm, D_MODEL), F32),
            pltpu.VMEM((tm, D_MODEL), BF16),
            pltpu.VMEM((tm, D_MODEL), F32),
        ],
        compiler_params=pltpu.CompilerParams(
            dimension_semantics=("arbitrary",), vmem_limit_bytes=VMEM_LIMIT),
        name="post_ffn_uv" if with_uv else "post_ffn",
    )(*operands)


def kernel(x, a_w_in, a_w_uk, a_w_uv, a_kv_norm_g, a_w_o, b_w_in, b_w_grp, b_scale, b_w_o,
           f_w_gu, f_w_down, ln_mix_g, ln_mix_b, ln_ffn_g, ln_ffn_b):
    bsz, seq, _ = x.shape
    t = bsz * seq
    x2 = x.reshape(t, D_MODEL)
    row = lambda v: v.reshape(1, -1)

    w_in = a_w_in[0]
    o_c = D_MODEL + KV_LATENT
    o_qi = o_c + IDX_HEADS * IDX_DIM
    o_ki = o_qi + IDX_DIM
    w1 = w_in.astype(BF16)
    wwit = w_in[:, o_ki:].T.astype(BF16)
    eye = jnp.eye(HEAD_GROUP, dtype=F32)
    ukt = jnp.swapaxes(a_w_uk[0], 1, 2).reshape(N_HEAD_GROUPS, HEAD_GROUP, HEAD_DIM, KV_LATENT)
    wukbd = jnp.einsum('ghdc,hk->ghdkc', ukt, eye).reshape(
        N_HEAD_GROUPS, HEAD_GROUP * HEAD_DIM, HEAD_GROUP * KV_LATENT).astype(BF16)
    uv = a_w_uv[0].reshape(N_HEAD_GROUPS, HEAD_GROUP, KV_LATENT, HEAD_DIM)
    wuvbd = jnp.einsum('ghcd,hk->ghckd', uv, eye).reshape(
        N_HEAD_GROUPS, HEAD_GROUP * KV_LATENT, HEAD_GROUP * HEAD_DIM).astype(BF16)

    qlat, c, ki, qit, wit = _proj_call(x2, w1, wwit, wukbd, row(a_kv_norm_g[0]), bsz, seq, 512)
    olat = _attn_call(qlat, c, ki, qit, wit, bsz, seq)
    h = _post_call(x2, olat.reshape(t, N_HEADS * KV_LATENT), wuvbd, a_w_o[0].astype(BF16),
                   row(ln_mix_g[0]), row(ln_mix_b[0]), f_w_gu[0].astype(BF16), f_w_down[0].astype(BF16),
                   row(ln_ffn_g[0]), row(ln_ffn_b[0]), 512)

    y = _pool_call(h.reshape(bsz, seq, D_MODEL), b_w_in[0].astype(BF16), b_w_grp[0].astype(BF16),
                   row(b_scale[0]), 512)
    h = _post_call(h, y.reshape(t, D_MODEL), None, b_w_o[0].astype(BF16),
                   row(ln_mix_g[1]), row(ln_mix_b[1]), f_w_gu[1].astype(BF16), f_w_down[1].astype(BF16),
                   row(ln_ffn_g[1]), row(ln_ffn_b[1]), 512)
    return h.reshape(bsz, seq, D_MODEL)
```

```python
import functools
import math

import jax
import jax.numpy as jnp
from jax import lax
from jax.experimental import pallas as pl
from jax.experimental.pallas import tpu as pltpu

D_MODEL = 1024
N_HEADS = 16
HEAD_DIM = 64
KV_LATENT = 256
IDX_HEADS = 8
IDX_DIM = 64
TOPK = 256
POOL_WINDOWS = (2, 4, 8, 16)
POOL_GROUP_DIM = 256
D_FF = 2816
DEPTH = 2
DEEPNORM_ALPHA = (2 * DEPTH) ** 0.25
LN_EPS = 1e-5
RMS_EPS = 1e-6
LOG2E = math.log2(math.e)

Q_TILE = 256
K_CHUNK = 256
HEAD_GROUP = 4
N_HEAD_GROUPS = N_HEADS // HEAD_GROUP
ATT_HEADS = 8
ATT_ROWS = ATT_HEADS * Q_TILE
FF_CHUNK = 256
N_FF_CHUNKS = D_FF // FF_CHUNK
POST_ROWS = 512
POOL_HALO = 16
MASK_BIAS = -1e30
N_PLAIN_SEARCH = 12
N_FIXED_SNAP = 4
MAX_SNAP_ITERS = 4096

BF16 = jnp.bfloat16
F32 = jnp.float32
VMEM_LIMIT = 56 * 1024 * 1024


def _nt_dot(a, b):
    return lax.dot_general(a, b, (((1,), (1,)), ((), ())), preferred_element_type=F32)


def _layer_norm(z, g, b):
    mu = jnp.mean(z, axis=-1, keepdims=True)
    zc = z - mu
    var = jnp.mean(zc * zc, axis=-1, keepdims=True)
    return zc * lax.rsqrt(var + LN_EPS) * g + b


def _fold_rows(x, op, parts=4):
    x = x.reshape(parts, -1, 8, x.shape[-1])
    y = op(x, axis=1)
    return op(y, axis=0)


def _lane_rep(x, op):
    return jnp.broadcast_to(op(x, axis=0, keepdims=True), x.shape)


def _proj_kernel(x_ref, w1_ref, wwit_ref, wukbd_ref, g_ref,
                 qlat_ref, c_ref, ki_ref, qit_ref, wit_ref):
    xb = x_ref[...].astype(BF16)
    o_c = D_MODEL + KV_LATENT
    o_qi = o_c + IDX_HEADS * IDX_DIM
    main = jnp.dot(xb, w1_ref[...], preferred_element_type=F32)
    ckv = main[:, D_MODEL:o_c]
    ms = jnp.mean(ckv * ckv, axis=-1, keepdims=True)
    c_ref[0] = (ckv * lax.rsqrt(ms + RMS_EPS) * g_ref[...]).astype(BF16)
    ki_ref[0] = main[:, o_qi:o_qi + IDX_DIM].astype(BF16)
    qscale = (HEAD_DIM ** -0.5) * LOG2E
    for g4 in range(N_HEAD_GROUPS):
        qg = main[:, g4 * 256:(g4 + 1) * 256].astype(BF16)
        ql = jnp.dot(qg, wukbd_ref[g4], preferred_element_type=F32) * qscale
        for hh in range(HEAD_GROUP):
            qlat_ref[0, g4 * HEAD_GROUP + hh] = ql[:, hh * KV_LATENT:(hh + 1) * KV_LATENT].astype(BF16)
    qit_ref[...] = (main[:, o_c:o_qi] * (IDX_DIM ** -0.5)).T.astype(BF16)
    wit_ref[...] = _nt_dot(wwit_ref[...], xb) * (IDX_HEADS ** -0.5)


def _proj_call(x2, w1, wwit, wukbd, g, bsz, seq, tm):
    t = bsz * seq
    per = seq // tm
    return pl.pallas_call(
        _proj_kernel,
        grid=(t // tm,),
        in_specs=[
            pl.BlockSpec((tm, D_MODEL), lambda i: (i, 0)),
            pl.BlockSpec(w1.shape, lambda i: (0, 0)),
            pl.BlockSpec(wwit.shape, lambda i: (0, 0)),
            pl.BlockSpec(wukbd.shape, lambda i: (0, 0, 0)),
            pl.BlockSpec(g.shape, lambda i: (0, 0)),
        ],
        out_specs=[
            pl.BlockSpec((1, N_HEADS, tm, KV_LATENT), lambda i: (i // per, 0, i % per, 0)),
            pl.BlockSpec((1, tm, KV_LATENT), lambda i: (i // per, i % per, 0)),
            pl.BlockSpec((1, tm, IDX_DIM), lambda i: (i // per, i % per, 0)),
            pl.BlockSpec((IDX_HEADS * IDX_DIM, tm), lambda i: (0, i)),
            pl.BlockSpec((IDX_HEADS, tm), lambda i: (0, i)),
        ],
        out_shape=[
            jax.ShapeDtypeStruct((bsz, N_HEADS, seq, KV_LATENT), BF16),
            jax.ShapeDtypeStruct((bsz, seq, KV_LATENT), BF16),
            jax.ShapeDtypeStruct((bsz, seq, IDX_DIM), BF16),
            jax.ShapeDtypeStruct((IDX_HEADS * IDX_DIM, t), BF16),
            jax.ShapeDtypeStruct((IDX_HEADS, t), F32),
        ],
        compiler_params=pltpu.CompilerParams(
            dimension_semantics=("arbitrary",), vmem_limit_bytes=VMEM_LIMIT),
        name="dsa_proj",
    )(x2, w1, wwit, wukbd, g)


def _attn_kernel(qlat_ref, c_ref, ki_ref, qit_ref, wit_ref, o_ref,
                 st_ref, dm_ref, lg_ref, acc_ref, mx_ref, sm_ref):
    j = pl.program_id(1)
    nkc = j + 1
    q0 = j * Q_TILE
    neg_inf = jnp.float32(-jnp.inf)
    pos_inf = jnp.float32(jnp.inf)
    vshape = (8, Q_TILE)

    def chunk_loop(fn, init):
        carry = lax.fori_loop(0, nkc // 2, lambda kp, c: fn(2 * kp + 1, fn(2 * kp, c)), init)
        return lax.cond(nkc % 2 == 1, lambda c: fn(nkc - 1, c), lambda c: c, carry)

    qi_all = jnp.concatenate(
        [qit_ref[h * IDX_DIM:(h + 1) * IDX_DIM, :] for h in range(IDX_HEADS)], axis=1)
    wit = wit_ref[...]
    qpos = q0 + lax.broadcasted_iota(jnp.int32, (K_CHUNK, Q_TILE), 1)
    krow = lax.broadcasted_iota(jnp.int32, (K_CHUNK, Q_TILE), 0)

    def score_body(kc, carry):
        mn, mxv = carry
        k0 = pl.multiple_of(kc * K_CHUNK, K_CHUNK)
        s_all = jnp.dot(ki_ref[0, pl.ds(k0, K_CHUNK), :], qi_all,
                        preferred_element_type=F32)
        s = wit[0:1, :] * jnp.maximum(s_all[:, 0:Q_TILE], 0.0)
        for h in range(1, IDX_HEADS):
            s = s + wit[h:h + 1, :] * jnp.maximum(s_all[:, h * Q_TILE:(h + 1) * Q_TILE], 0.0)
        valid = (k0 + krow) <= qpos
        st_ref[pl.ds(k0, K_CHUNK), :] = jnp.where(valid, s, neg_inf)
        mn = jnp.minimum(mn, _fold_rows(jnp.where(valid, s, pos_inf), jnp.min))
        mxv = jnp.maximum(mxv, _fold_rows(jnp.where(valid, s, neg_inf), jnp.max))
        return mn, mxv

    mn, mxv = chunk_loop(score_body, (jnp.full(vshape, pos_inf, F32), jnp.full(vshape, neg_inf, F32)))
    rowmin = _lane_rep(mn, jnp.min)
    rowmax = _lane_rep(mxv, jnp.max)

    topk_f = jnp.float32(TOPK)

    def score_block(kc):
        k0 = pl.multiple_of(kc * K_CHUNK, K_CHUNK)
        return st_ref[pl.ds(k0, K_CHUNK), :]

    def count_ge(thr):
        def body(kc, acc):
            return acc + _fold_rows(jnp.where(score_block(kc) >= thr[0:1, :], 1.0, 0.0), jnp.sum)
        return _lane_rep(chunk_loop(body, jnp.zeros(vshape, F32)), jnp.sum)

    def count_snap(thr):
        def body(kc, carry):
            cnt, amin, bmx = carry
            blk = score_block(kc)
            ge = blk >= thr[0:1, :]
            cnt = cnt + _fold_rows(jnp.where(ge, 1.0, 0.0), jnp.sum)
            amin = jnp.minimum(amin, _fold_rows(jnp.where(ge, blk, pos_inf), jnp.min))
            bmx = jnp.maximum(bmx, _fold_rows(jnp.where(ge, neg_inf, blk), jnp.max))
            return cnt, amin, bmx
        cnt, amin, bmx = chunk_loop(
            body, (jnp.zeros(vshape, F32), jnp.full(vshape, pos_inf, F32), jnp.full(vshape, neg_inf, F32)))
        return _lane_rep(cnt, jnp.sum), _lane_rep(amin, jnp.min), _lane_rep(bmx, jnp.max)

    def unsettled(clo, bmin, bmax):
        return jnp.logical_and(clo > topk_f, bmin < bmax)

    def probe_point(lo, hi, bmin, bmax):
        a = jnp.maximum(lo, bmin)
        b = jnp.minimum(hi, bmax)
        mid = a * 0.5 + b * 0.5
        return jnp.where(mid <= lo, b, mid)

    def plain_body(i, state):
        lo, hi, clo, chi, bmin, bmax = state
        active = unsettled(clo, bmin, bmax)
        mid = jnp.where(i == 0, rowmax, probe_point(lo, hi, bmin, bmax))
        c = count_ge(mid)
        up = jnp.logical_and(active, c >= topk_f)
        dn = jnp.logical_and(active, c < topk_f)
        return (jnp.where(up, mid, lo), jnp.where(dn, mid, hi),
                jnp.where(up, c, clo), jnp.where(dn, c, chi), bmin, bmax)

    n_valid = (q0 + lax.broadcasted_iota(jnp.int32, vshape, 1) + 1).astype(F32)
    state = (rowmin, jnp.full(vshape, pos_inf, F32), n_valid, jnp.zeros(vshape, F32), rowmin, rowmax)
    state = lax.fori_loop(0, N_PLAIN_SEARCH, plain_body, state)

    def snap_flag(clo, bmin, bmax):
        return jnp.max(jnp.where(unsettled(clo, bmin, bmax), 1.0, 0.0))

    def snap_cond(carry):
        it, flag = carry[0], carry[1]
        return jnp.logical_and(flag > 0.0, it < MAX_SNAP_ITERS)

    def snap_step(i, state):
        lo, hi, clo, chi, bmin, bmax = state
        active = unsettled(clo, bmin, bmax)
        mid = probe_point(lo, hi, bmin, bmax)
        c, amin, bmx = count_snap(mid)
        up = jnp.logical_and(active, c >= topk_f)
        dn = jnp.logical_and(active, c < topk_f)
        return (jnp.where(up, mid, lo), jnp.where(dn, mid, hi), jnp.where(up, c, clo),
                jnp.where(dn, c, chi), jnp.where(up, amin, bmin), jnp.where(dn, bmx, bmax))

    def snap_body(carry):
        state = snap_step(0, carry[2:])
        return (carry[0] + 1, snap_flag(state[2], state[4], state[5])) + state

    state = lax.fori_loop(0, N_FIXED_SNAP, snap_step, state)
    carry = lax.while_loop(snap_cond, snap_body,
                           (jnp.int32(0), snap_flag(state[2], state[4], state[5])) + state)
    _, _, lo, hi, clo, chi, bmin, _ = carry

    tie_r = clo[0:1, :] > topk_f
    thr_r = jnp.where(tie_r, hi[0:1, :], lo[0:1, :])
    tv_r = bmin[0:1, :]
    need_r = topk_f - chi[0:1, :]
    tri = (lax.broadcasted_iota(jnp.int32, (K_CHUNK, K_CHUNK), 1)
           < lax.broadcasted_iota(jnp.int32, (K_CHUNK, K_CHUNK), 0)).astype(BF16)

    def mask_chunk(kc, run):
        k0 = pl.multiple_of(kc * K_CHUNK, K_CHUNK)
        blk = st_ref[pl.ds(k0, K_CHUNK), :]
        sel = blk >= thr_r
        eq = jnp.logical_and(tie_r, blk == tv_r)
        eq_f = jnp.where(eq, 1.0, 0.0)
        rank = jnp.dot(tri, eq_f.astype(BF16), preferred_element_type=F32) + run
        sel = jnp.logical_or(sel, jnp.logical_and(eq, rank < need_r))
        dist = ((k0 + krow) - qpos).astype(F32)
        dm_ref[kc] = jnp.where(sel, dist, MASK_BIAS).T
        return run + jnp.sum(eq_f, axis=0, keepdims=True)

    chunk_loop(mask_chunk, jnp.zeros((1, Q_TILE), F32))

    def fold_lanes(x, op):
        parts = [x[:, t * 128:(t + 1) * 128] for t in range(x.shape[1] // 128)]
        while len(parts) > 1:
            parts = [op(parts[2 * t], parts[2 * t + 1]) for t in range(len(parts) // 2)]
        return parts[0]

    for g in range(N_HEADS // ATT_HEADS):
        heads = range(g * ATT_HEADS, (g + 1) * ATT_HEADS)
        ql = qlat_ref[0, g * ATT_HEADS:(g + 1) * ATT_HEADS].reshape(ATT_ROWS, KV_LATENT)
        slopes = [LOG2E * 2.0 ** (-8.0 * (h + 1) / N_HEADS) for h in heads]
        mx_ref[...] = jnp.full(mx_ref.shape, neg_inf, F32)

        def logit_block(kc0, nch, ql=ql, slopes=slopes):
            k0 = pl.multiple_of(kc0 * K_CHUNK, K_CHUNK)
            lg = _nt_dot(ql, c_ref[0, pl.ds(k0, nch * K_CHUNK), :])
            dmk = jnp.concatenate([dm_ref[kc0 + t] for t in range(nch)], axis=1)
            for hh in range(ATT_HEADS):
                rs = slice(hh * Q_TILE, (hh + 1) * Q_TILE)
                lgh = lg[rs] + slopes[hh] * dmk
                for t in range(nch):
                    lg_ref[kc0 + t, rs, :] = lgh[:, t * K_CHUNK:(t + 1) * K_CHUNK]
                mx_ref[rs, :] = jnp.maximum(mx_ref[rs, :], fold_lanes(lgh, jnp.maximum))

        def logit_body(kp, _, logit_block=logit_block):
            logit_block(2 * kp, 2)
            return 0

        lax.fori_loop(0, nkc // 2, logit_body, 0)

        @pl.when(nkc % 2 == 1)
        def _(logit_block=logit_block):
            logit_block(nkc - 1, 1)

        m = jnp.max(mx_ref[...], axis=-1, keepdims=True)

        acc_ref[...] = jnp.zeros(acc_ref.shape, F32)
        sm_ref[...] = jnp.zeros(sm_ref.shape, F32)

        def pv_block(kc0, nch, m=m):
            k0 = pl.multiple_of(kc0 * K_CHUNK, K_CHUNK)
            p = jnp.concatenate([jnp.exp2(lg_ref[kc0 + t] - m) for t in range(nch)], axis=1)
            sm_ref[...] += fold_lanes(p, jnp.add)
            acc_ref[...] += jnp.dot(p.astype(BF16), c_ref[0, pl.ds(k0, nch * K_CHUNK), :],
                                    preferred_element_type=F32)

        def pv_body(kp, _, pv_block=pv_block):
            pv_block(2 * kp, 2)
            return 0

        lax.fori_loop(0, nkc // 2, pv_body, 0)

        @pl.when(nkc % 2 == 1)
        def _(pv_block=pv_block):
            pv_block(nkc - 1, 1)

        inv = 1.0 / jnp.sum(sm_ref[...], axis=-1, keepdims=True)
        for hh, h in enumerate(heads):
            rs = slice(hh * Q_TILE, (hh + 1) * Q_TILE)
            o_ref[0, :, h * KV_LATENT:(h + 1) * KV_LATENT] = (acc_ref[rs, :] * inv[rs]).astype(BF16)


def _attn_call(qlat, c, ki, qit, wit, bsz, seq):
    nq = seq // Q_TILE
    n_chunks = seq // K_CHUNK
    return pl.pallas_call(
        _attn_kernel,
        grid=(bsz, nq),
        in_specs=[
            pl.BlockSpec((1, N_HEADS, Q_TILE, KV_LATENT), lambda b, j: (b, 0, j, 0)),
            pl.BlockSpec((1, seq, KV_LATENT), lambda b, j: (b, 0, 0)),
            pl.BlockSpec((1, seq, IDX_DIM), lambda b, j: (b, 0, 0)),
            pl.BlockSpec((IDX_HEADS * IDX_DIM, Q_TILE), lambda b, j: (0, b * nq + j)),
            pl.BlockSpec((IDX_HEADS, Q_TILE), lambda b, j: (0, b * nq + j)),
        ],
        out_specs=pl.BlockSpec((1, Q_TILE, N_HEADS * KV_LATENT), lambda b, j: (b, j, 0)),
        out_shape=jax.ShapeDtypeStruct((bsz, seq, N_HEADS * KV_LATENT), BF16),
        scratch_shapes=[
            pltpu.VMEM((seq, Q_TILE), F32),
            pltpu.VMEM((n_chunks, Q_TILE, K_CHUNK), F32),
            pltpu.VMEM((n_chunks, ATT_ROWS, K_CHUNK), F32),
            pltpu.VMEM((ATT_ROWS, KV_LATENT), F32),
            pltpu.VMEM((ATT_ROWS, 128), F32),
            pltpu.VMEM((ATT_ROWS, 128), F32),
        ],
        compiler_params=pltpu.CompilerParams(
            dimension_semantics=("arbitrary", "arbitrary"), vmem_limit_bytes=VMEM_LIMIT),
        name="dsa_attn",
    )(qlat, c, ki, qit, wit)


def _pool_kernel(h_ref, halo_ref, win_ref, wgrp_ref, scale_ref, y_ref):
    i = pl.program_id(1)
    tm = h_ref.shape[1]
    u_main = jnp.dot(h_ref[0].astype(BF16), win_ref[...], preferred_element_type=F32)
    u_halo = jnp.dot(halo_ref[0].astype(BF16), win_ref[...], preferred_element_type=F32)
    u_halo = u_halo * jnp.where(i > 0, 1.0, 0.0)
    u = jnp.concatenate([u_halo, u_main], axis=0)
    pos = i * tm + lax.broadcasted_iota(jnp.int32, (tm, 1), 0)
    for g, window in enumerate(POOL_WINDOWS):
        ug = u[:, g * POOL_GROUP_DIM:(g + 1) * POOL_GROUP_DIM]
        s = ug
        shift = 1
        while shift < window:
            s = s + pltpu.roll(s, shift, axis=0)
            shift *= 2
        cnt = jnp.minimum(pos + 1, window).astype(F32)
        pooled = (s[POOL_HALO:] / cnt - ug[POOL_HALO:]).astype(BF16)
        yg = jnp.dot(pooled, wgrp_ref[g], preferred_element_type=F32)
        yg = yg * scale_ref[:, g * POOL_GROUP_DIM:(g + 1) * POOL_GROUP_DIM]
        y_ref[0, :, g * POOL_GROUP_DIM:(g + 1) * POOL_GROUP_DIM] = yg.astype(BF16)


def _pool_call(h3, win, wgrp, scale, tm):
    bsz, seq, _ = h3.shape
    halo_per = tm // POOL_HALO
    return pl.pallas_call(
        _pool_kernel,
        grid=(bsz, seq // tm),
        in_specs=[
            pl.BlockSpec((1, tm, D_MODEL), lambda b, i: (b, i, 0)),
            pl.BlockSpec((1, POOL_HALO, D_MODEL), lambda b, i: (b, jnp.maximum(i * halo_per - 1, 0), 0)),
            pl.BlockSpec(win.shape, lambda b, i: (0, 0)),
            pl.BlockSpec(wgrp.shape, lambda b, i: (0, 0, 0)),
            pl.BlockSpec(scale.shape, lambda b, i: (0, 0)),
        ],
        out_specs=pl.BlockSpec((1, tm, D_MODEL), lambda b, i: (b, i, 0)),
        out_shape=jax.ShapeDtypeStruct((bsz, seq, D_MODEL), BF16),
        compiler_params=pltpu.CompilerParams(
            dimension_semantics=("arbitrary", "arbitrary"), vmem_limit_bytes=VMEM_LIMIT),
        name="pool_mix",
    )(h3, h3, win, wgrp, scale)


def _post_kernel(*refs, with_uv):
    if with_uv:
        (h_ref, mi_ref, wuv_ref, wmo_ref, g1_ref, b1_ref, wgu_ref, wd_ref, g2_ref, b2_ref,
         out_ref, h1_ref, hb_ref, acc_ref) = refs
    else:
        (h_ref, mi_ref, wmo_ref, g1_ref, b1_ref, wgu_ref, wd_ref, g2_ref, b2_ref,
         out_ref, h1_ref, hb_ref, acc_ref) = refs
    rows = POST_ROWS
    group = HEAD_GROUP * KV_LATENT

    for rs in [slice(r * rows, (r + 1) * rows) for r in range(h_ref.shape[0] // rows)]:
        if with_uv:
            mi = jnp.concatenate(
                [jnp.dot(mi_ref[rs, g * group:(g + 1) * group], wuv_ref[g],
                         preferred_element_type=F32).astype(BF16) for g in range(N_HEAD_GROUPS)], axis=1)
        else:
            mi = mi_ref[rs, :]
        mix = jnp.dot(mi, wmo_ref[...], preferred_element_type=F32)
        h1 = _layer_norm(DEEPNORM_ALPHA * h_ref[rs, :] + mix, g1_ref[...], b1_ref[...])
        h1_ref[rs, :] = h1
        hb_ref[rs, :] = h1.astype(BF16)
        acc_ref[rs, :] = jnp.zeros((rows, D_MODEL), F32)

        for f in range(N_FF_CHUNKS):
            hb = hb_ref[rs, :]
            cols = slice(f * FF_CHUNK, (f + 1) * FF_CHUNK)
            gate = jnp.dot(hb, wgu_ref[:, cols], preferred_element_type=F32)
            up = jnp.dot(hb, wgu_ref[:, D_FF + f * FF_CHUNK:D_FF + (f + 1) * FF_CHUNK],
                         preferred_element_type=F32)
            act = (gate * (1.0 / (1.0 + jnp.exp(-gate))) * up).astype(BF16)
            acc_ref[rs, :] += jnp.dot(act, wd_ref[cols, :], preferred_element_type=F32)
        out_ref[rs, :] = _layer_norm(DEEPNORM_ALPHA * h1_ref[rs, :] + acc_ref[rs, :],
                                     g2_ref[...], b2_ref[...])


def _post_call(h2, mi2, wuv, wmo, g1, b1, wgu, wd, g2, b2, tm):
    t = h2.shape[0]
    const2 = lambda i: (0, 0)
    const3 = lambda i: (0, 0, 0)
    resident = dict(pipeline_mode=pl.Buffered(1))
    with_uv = wuv is not None
    in_specs = [
        pl.BlockSpec((tm, D_MODEL), lambda i: (i, 0)),
        pl.BlockSpec((tm, mi2.shape[1]), lambda i: (i, 0)),
    ]
    operands = [h2, mi2]
    if with_uv:
        in_specs.append(pl.BlockSpec(wuv.shape, const3, **resident))
        operands.append(wuv)
    in_specs += [
        pl.BlockSpec(wmo.shape, const2, **resident),
        pl.BlockSpec(g1.shape, const2),
        pl.BlockSpec(b1.shape, const2),
        pl.BlockSpec(wgu.shape, const2, **resident),
        pl.BlockSpec(wd.shape, const2, **resident),
        pl.BlockSpec(g2.shape, const2),
        pl.BlockSpec(b2.shape, const2),
    ]
    operands += [wmo, g1, b1, wgu, wd, g2, b2]
    return pl.pallas_call(
        functools.partial(_post_kernel, with_uv=with_uv),
        grid=(t // tm,),
        in_specs=in_specs,
        out_specs=pl.BlockSpec((tm, D_MODEL), lambda i: (i, 0)),
        out_shape=jax.ShapeDtypeStruct((t, D_MODEL), F32),
        scratch_shapes=[
            pltpu.VMEM((tm, D_MODEL), F32),
            pltpu.VMEM((tm, D_MODEL), BF16),
            pltpu.VMEM((tm, D_MODEL), F32),
        ],
        compiler_params=pltpu.CompilerParams(
            dimension_semantics=("arbitrary",), vmem_limit_bytes=VMEM_LIMIT),
        name="post_ffn_uv" if with_uv else "post_ffn",
    )(*operands)


def kernel(x, a_w_in, a_w_uk, a_w_uv, a_kv_norm_g, a_w_o, b_w_in, b_w_grp, b_scale, b_w_o,
           f_w_gu, f_w_down, ln_mix_g, ln_mix_b, ln_ffn_g, ln_ffn_b):
    bsz, seq, _ = x.shape
    t = bsz * seq
    x2 = x.reshape(t, D_MODEL)
    row = lambda v: v.reshape(1, -1)

    w_in = a_w_in[0]
    o_c = D_MODEL + KV_LATENT
    o_qi = o_c + IDX_HEADS * IDX_DIM
    o_ki = o_qi + IDX_DIM
    w1 = w_in.astype(BF16)
    wwit = w_in[:, o_ki:].T.astype(BF16)
    eye = jnp.eye(HEAD_GROUP, dtype=F32)
    ukt = jnp.swapaxes(a_w_uk[0], 1, 2).reshape(N_HEAD_GROUPS, HEAD_GROUP, HEAD_DIM, KV_LATENT)
    wukbd = jnp.einsum('ghdc,hk->ghdkc', ukt, eye).reshape(
        N_HEAD_GROUPS, HEAD_GROUP * HEAD_DIM, HEAD_GROUP * KV_LATENT).astype(BF16)
    uv = a_w_uv[0].reshape(N_HEAD_GROUPS, HEAD_GROUP, KV_LATENT, HEAD_DIM)
    wuvbd = jnp.einsum('ghcd,hk->ghckd', uv, eye).reshape(
        N_HEAD_GROUPS, HEAD_GROUP * KV_LATENT, HEAD_GROUP * HEAD_DIM).astype(BF16)

    qlat, c, ki, qit, wit = _proj_call(x2, w1, wwit, wukbd, row(a_kv_norm_g[0]), bsz, seq, 512)
    olat = _attn_call(qlat, c, ki, qit, wit, bsz, seq)
    h = _post_call(x2, olat.reshape(t, N_HEADS * KV_LATENT), wuvbd, a_w_o[0].astype(BF16),
                   row(ln_mix_g[0]), row(ln_mix_b[0]), f_w_gu[0].astype(BF16), f_w_down[0].astype(BF16),
                   row(ln_ffn_g[0]), row(ln_ffn_b[0]), 512)

    y = _pool_call(h.reshape(bsz, seq, D_MODEL), b_w_in[0].astype(BF16), b_w_grp[0].astype(BF16),
                   row(b_scale[0]), 512)
    h = _post_call(h, y.reshape(t, D_MODEL), None, b_w_o[0].astype(BF16),
                   row(ln_mix_g[1]), row(ln_mix_b[1]), f_w_gu[1].astype(BF16), f_w_down[1].astype(BF16),
                   row(ln_ffn_g[1]), row(ln_ffn_b[1]), 512)
    return h.reshape(bsz, seq, D_MODEL)
```

```python
import functools
import math

import jax
import jax.numpy as jnp
from jax import lax
from jax.experimental import pallas as pl
from jax.experimental.pallas import tpu as pltpu

D_MODEL = 1024
N_HEADS = 16
HEAD_DIM = 64
KV_LATENT = 256
IDX_HEADS = 8
IDX_DIM = 64
TOPK = 256
POOL_WINDOWS = (2, 4, 8, 16)
POOL_GROUP_DIM = 256
D_FF = 2816
DEPTH = 2
DEEPNORM_ALPHA = (2 * DEPTH) ** 0.25
LN_EPS = 1e-5
RMS_EPS = 1e-6
LOG2E = math.log2(math.e)

Q_TILE = 256
K_CHUNK = 256
HEAD_GROUP = 4
N_HEAD_GROUPS = N_HEADS // HEAD_GROUP
ATT_HEADS = 8
ATT_ROWS = ATT_HEADS * Q_TILE
FF_CHUNK = 256
N_FF_CHUNKS = D_FF // FF_CHUNK
POST_ROWS = 512
POOL_HALO = 16
MASK_BIAS = -1e30
N_PLAIN_SEARCH = 12
N_FIXED_SNAP = 4
MAX_SNAP_ITERS = 4096

BF16 = jnp.bfloat16
F32 = jnp.float32
VMEM_LIMIT = 56 * 1024 * 1024


def _nt_dot(a, b):
    return lax.dot_general(a, b, (((1,), (1,)), ((), ())), preferred_element_type=F32)


def _layer_norm(z, g, b):
    mu = jnp.mean(z, axis=-1, keepdims=True)
    zc = z - mu
    var = jnp.mean(zc * zc, axis=-1, keepdims=True)
    return zc * lax.rsqrt(var + LN_EPS) * g + b


def _fold_rows(x, op, parts=4):
    x = x.reshape(parts, -1, 8, x.shape[-1])
    y = op(x, axis=1)
    return op(y, axis=0)


def _lane_rep(x, op):
    return jnp.broadcast_to(op(x, axis=0, keepdims=True), x.shape)


def _proj_kernel(x_ref, w1_ref, wwit_ref, wukbd_ref, g_ref,
                 qlat_ref, c_ref, ki_ref, qit_ref, wit_ref):
    xb = x_ref[...].astype(BF16)
    o_c = D_MODEL + KV_LATENT
    o_qi = o_c + IDX_HEADS * IDX_DIM
    main = jnp.dot(xb, w1_ref[...], preferred_element_type=F32)
    ckv = main[:, D_MODEL:o_c]
    ms = jnp.mean(ckv * ckv, axis=-1, keepdims=True)
    c_ref[0] = (ckv * lax.rsqrt(ms + RMS_EPS) * g_ref[...]).astype(BF16)
    ki_ref[0] = main[:, o_qi:o_qi + IDX_DIM].astype(BF16)
    qscale = (HEAD_DIM ** -0.5) * LOG2E
    for g4 in range(N_HEAD_GROUPS):
        qg = main[:, g4 * 256:(g4 + 1) * 256].astype(BF16)
        ql = jnp.dot(qg, wukbd_ref[g4], preferred_element_type=F32) * qscale
        for hh in range(HEAD_GROUP):
            qlat_ref[0, g4 * HEAD_GROUP + hh] = ql[:, hh * KV_LATENT:(hh + 1) * KV_LATENT].astype(BF16)
    qit_ref[...] = (main[:, o_c:o_qi] * (IDX_DIM ** -0.5)).T.astype(BF16)
    wit_ref[...] = _nt_dot(wwit_ref[...], xb) * (IDX_HEADS ** -0.5)


def _proj_call(x2, w1, wwit, wukbd, g, bsz, seq, tm):
    t = bsz * seq
    per = seq // tm
    return pl.pallas_call(
        _proj_kernel,
        grid=(t // tm,),
        in_specs=[
            pl.BlockSpec((tm, D_MODEL), lambda i: (i, 0)),
            pl.BlockSpec(w1.shape, lambda i: (0, 0)),
            pl.BlockSpec(wwit.shape, lambda i: (0, 0)),
            pl.BlockSpec(wukbd.shape, lambda i: (0, 0, 0)),
            pl.BlockSpec(g.shape, lambda i: (0, 0)),
        ],
        out_specs=[
            pl.BlockSpec((1, N_HEADS, tm, KV_LATENT), lambda i: (i // per, 0, i % per, 0)),
            pl.BlockSpec((1, tm, KV_LATENT), lambda i: (i // per, i % per, 0)),
            pl.BlockSpec((1, tm, IDX_DIM), lambda i: (i // per, i % per, 0)),
            pl.BlockSpec((IDX_HEADS * IDX_DIM, tm), lambda i: (0, i)),
            pl.BlockSpec((IDX_HEADS, tm), lambda i: (0, i)),
        ],
        out_shape=[
            jax.ShapeDtypeStruct((bsz, N_HEADS, seq, KV_LATENT), BF16),
            jax.ShapeDtypeStruct((bsz, seq, KV_LATENT), BF16),
            jax.ShapeDtypeStruct((bsz, seq, IDX_DIM), BF16),
            jax.ShapeDtypeStruct((IDX_HEADS * IDX_DIM, t), BF16),
            jax.ShapeDtypeStruct((IDX_HEADS, t), F32),
        ],
        compiler_params=pltpu.CompilerParams(
            dimension_semantics=("arbitrary",), vmem_limit_bytes=VMEM_LIMIT),
        name="dsa_proj",
    )(x2, w1, wwit, wukbd, g)


def _attn_kernel(qlat_ref, c_ref, ki_ref, qit_ref, wit_ref, o_ref,
                 st_ref, dm_ref, lg_ref, acc_ref, mx_ref, sm_ref):
    j = pl.program_id(1)
    nkc = j + 1
    q0 = j * Q_TILE
    neg_inf = jnp.float32(-jnp.inf)
    pos_inf = jnp.float32(jnp.inf)
    vshape = (8, Q_TILE)

    def chunk_loop(fn, init):
        carry = lax.fori_loop(0, nkc // 2, lambda kp, c: fn(2 * kp + 1, fn(2 * kp, c)), init)
        return lax.cond(nkc % 2 == 1, lambda c: fn(nkc - 1, c), lambda c: c, carry)

    qi_all = jnp.concatenate(
        [qit_ref[h * IDX_DIM:(h + 1) * IDX_DIM, :] for h in range(IDX_HEADS)], axis=1)
    wit = wit_ref[...]
    qpos = q0 + lax.broadcasted_iota(jnp.int32, (K_CHUNK, Q_TILE), 1)
    krow = lax.broadcasted_iota(jnp.int32, (K_CHUNK, Q_TILE), 0)

    def score_body(kc, carry):
        mn, mxv = carry
        k0 = pl.multiple_of(kc * K_CHUNK, K_CHUNK)
        s_all = jnp.dot(ki_ref[0, pl.ds(k0, K_CHUNK), :], qi_all,
                        preferred_element_type=F32)
        s = wit[0:1, :] * jnp.maximum(s_all[:, 0:Q_TILE], 0.0)
        for h in range(1, IDX_HEADS):
            s = s + wit[h:h + 1, :] * jnp.maximum(s_all[:, h * Q_TILE:(h + 1) * Q_TILE], 0.0)
        valid = (k0 + krow) <= qpos
        st_ref[pl.ds(k0, K_CHUNK), :] = jnp.where(valid, s, neg_inf)
        mn = jnp.minimum(mn, _fold_rows(jnp.where(valid, s, pos_inf), jnp.min))
        mxv = jnp.maximum(mxv, _fold_rows(jnp.where(valid, s, neg_inf), jnp.max))
        return mn, mxv

    mn, mxv = chunk_loop(score_body, (jnp.full(vshape, pos_inf, F32), jnp.full(vshape, neg_inf, F32)))
    rowmin = _lane_rep(mn, jnp.min)
    rowmax = _lane_rep(mxv, jnp.max)

    topk_f = jnp.float32(TOPK)

    def score_block(kc):
        k0 = pl.multiple_of(kc * K_CHUNK, K_CHUNK)
        return st_ref[pl.ds(k0, K_CHUNK), :]

    def count_ge(thr):
        def body(kc, acc):
            return acc + _fold_rows(jnp.where(score_block(kc) >= thr[0:1, :], 1.0, 0.0), jnp.sum)
        return _lane_rep(chunk_loop(body, jnp.zeros(vshape, F32)), jnp.sum)

    def count_snap(thr):
        def body(kc, carry):
            cnt, amin, bmx = carry
            blk = score_block(kc)
            ge = blk >= thr[0:1, :]
            cnt = cnt + _fold_rows(jnp.where(ge, 1.0, 0.0), jnp.sum)
            amin = jnp.minimum(amin, _fold_rows(jnp.where(ge, blk, pos_inf), jnp.min))
            bmx = jnp.maximum(bmx, _fold_rows(jnp.where(ge, neg_inf, blk), jnp.max))
            return cnt, amin, bmx
        cnt, amin, bmx = chunk_loop(
            body, (jnp.zeros(vshape, F32), jnp.full(vshape, pos_inf, F32), jnp.full(vshape, neg_inf, F32)))
        return _lane_rep(cnt, jnp.sum), _lane_rep(amin, jnp.min), _lane_rep(bmx, jnp.max)

    def unsettled(clo, bmin, bmax):
        return jnp.logical_and(clo > topk_f, bmin < bmax)

    def probe_point(lo, hi, bmin, bmax):
        a = jnp.maximum(lo, bmin)
        b = jnp.minimum(hi, bmax)
        mid = a * 0.5 + b * 0.5
        return jnp.where(mid <= lo, b, mid)

    def plain_body(i, state):
        lo, hi, clo, chi, bmin, bmax = state
        active = unsettled(clo, bmin, bmax)
        mid = jnp.where(i == 0, rowmax, probe_point(lo, hi, bmin, bmax))
        c = count_ge(mid)
        up = jnp.logical_and(active, c >= topk_f)
        dn = jnp.logical_and(active, c < topk_f)
        return (jnp.where(up, mid, lo), jnp.where(dn, mid, hi),
                jnp.where(up, c, clo), jnp.where(dn, c, chi), bmin, bmax)

    n_valid = (q0 + lax.broadcasted_iota(jnp.int32, vshape, 1) + 1).astype(F32)
    state = (rowmin, jnp.full(vshape, pos_inf, F32), n_valid, jnp.zeros(vshape, F32), rowmin, rowmax)
    state = lax.fori_loop(0, N_PLAIN_SEARCH, plain_body, state)

    def snap_flag(clo, bmin, bmax):
        return jnp.max(jnp.where(unsettled(clo, bmin, bmax), 1.0, 0.0))

    def snap_cond(carry):
        it, flag = carry[0], carry[1]
        return jnp.logical_and(flag > 0.0, it < MAX_SNAP_ITERS)

    def snap_step(i, state):
        lo, hi, clo, chi, bmin, bmax = state
        active = unsettled(clo, bmin, bmax)
        mid = probe_point(lo, hi, bmin, bmax)
        c, amin, bmx = count_snap(mid)
        up = jnp.logical_and(active, c >= topk_f)
        dn = jnp.logical_and(active, c < topk_f)
        return (jnp.where(up, mid, lo), jnp.where(dn, mid, hi), jnp.where(up, c, clo),
                jnp.where(dn, c, chi), jnp.where(up, amin, bmin), jnp.where(dn, bmx, bmax))

    def snap_body(carry):
        state = snap_step(0, carry[2:])
        return (carry[0] + 1, snap_flag(state[2], state[4], state[5])) + state

    state = lax.fori_loop(0, N_FIXED_SNAP, snap_step, state)
    carry = lax.while_loop(snap_cond, snap_body,
                           (jnp.int32(0), snap_flag(state[2], state[4], state[5])) + state)
    _, _, lo, hi, clo, chi, bmin, _ = carry

    tie_r = clo[0:1, :] > topk_f
    thr_r = jnp.where(tie_r, hi[0:1, :], lo[0:1, :])
    tv_r = bmin[0:1, :]
    need_r = topk_f - chi[0:1, :]
    tri = (lax.broadcasted_iota(jnp.int32, (K_CHUNK, K_CHUNK), 1)
           < lax.broadcasted_iota(jnp.int32, (K_CHUNK, K_CHUNK), 0)).astype(BF16)

    def mask_chunk(kc, run):
        k0 = pl.multiple_of(kc * K_CHUNK, K_CHUNK)
        blk = st_ref[pl.ds(k0, K_CHUNK), :]
        sel = blk >= thr_r
        eq = jnp.logical_and(tie_r, blk == tv_r)
        eq_f = jnp.where(eq, 1.0, 0.0)
        rank = jnp.dot(tri, eq_f.astype(BF16), preferred_element_type=F32) + run
        sel = jnp.logical_or(sel, jnp.logical_and(eq, rank < need_r))
        dist = ((k0 + krow) - qpos).astype(F32)
        dm_ref[kc] = jnp.where(sel, dist, MASK_BIAS).T
        return run + jnp.sum(eq_f, axis=0, keepdims=True)

    chunk_loop(mask_chunk, jnp.zeros((1, Q_TILE), F32))

    def fold_lanes(x, op):
        parts = [x[:, t * 128:(t + 1) * 128] for t in range(x.shape[1] // 128)]
        while len(parts) > 1:
            parts = [op(parts[2 * t], parts[2 * t + 1]) for t in range(len(parts) // 2)]
        return parts[0]

    def block_loop(block):
        def quad(kq, _):
            block(4 * kq, 4)
            return 0

        lax.fori_loop(0, nkc // 4, quad, 0)

        @pl.when((nkc // 2) % 2 == 1)
        def _():
            block(4 * (nkc // 4), 2)

        @pl.when(nkc % 2 == 1)
        def _():
            block(nkc - 1, 1)

    for g in range(N_HEADS // ATT_HEADS):
        heads = range(g * ATT_HEADS, (g + 1) * ATT_HEADS)
        ql = qlat_ref[0, g * ATT_HEADS:(g + 1) * ATT_HEADS].reshape(ATT_ROWS, KV_LATENT)
        slopes = [LOG2E * 2.0 ** (-8.0 * (h + 1) / N_HEADS) for h in heads]
        mx_ref[...] = jnp.full(mx_ref.shape, neg_inf, F32)

        def logit_block(kc0, nch, ql=ql, slopes=slopes):
            k0 = pl.multiple_of(kc0 * K_CHUNK, K_CHUNK)
            lg = _nt_dot(ql, c_ref[0, pl.ds(k0, nch * K_CHUNK), :])
            dmk = jnp.concatenate([dm_ref[kc0 + t] for t in range(nch)], axis=1)
            for hh in range(ATT_HEADS):
                rs = slice(hh * Q_TILE, (hh + 1) * Q_TILE)
                lgh = lg[rs] + slopes[hh] * dmk
                for t in range(nch):
                    lg_ref[kc0 + t, rs, :] = lgh[:, t * K_CHUNK:(t + 1) * K_CHUNK]
                mx_ref[rs, :] = jnp.maximum(mx_ref[rs, :], fold_lanes(lgh, jnp.maximum))

        block_loop(logit_block)

        m = jnp.max(mx_ref[...], axis=-1, keepdims=True)

        acc_ref[...] = jnp.zeros(acc_ref.shape, F32)
        sm_ref[...] = jnp.zeros(sm_ref.shape, F32)

        def pv_block(kc0, nch, m=m):
            k0 = pl.multiple_of(kc0 * K_CHUNK, K_CHUNK)
            p = jnp.concatenate([jnp.exp2(lg_ref[kc0 + t] - m) for t in range(nch)], axis=1)
            sm_ref[...] += fold_lanes(p, jnp.add)
            acc_ref[...] += jnp.dot(p.astype(BF16), c_ref[0, pl.ds(k0, nch * K_CHUNK), :],
                                    preferred_element_type=F32)

        block_loop(pv_block)

        inv = 1.0 / jnp.sum(sm_ref[...], axis=-1, keepdims=True)
        for hh, h in enumerate(heads):
            rs = slice(hh * Q_TILE, (hh + 1) * Q_TILE)
            o_ref[0, :, h * KV_LATENT:(h + 1) * KV_LATENT] = (acc_ref[rs, :] * inv[rs]).astype(BF16)


def _attn_call(qlat, c, ki, qit, wit, bsz, seq):
    nq = seq // Q_TILE
    n_chunks = seq // K_CHUNK
    return pl.pallas_call(
        _attn_kernel,
        grid=(bsz, nq),
        in_specs=[
            pl.BlockSpec((1, N_HEADS, Q_TILE, KV_LATENT), lambda b, j: (b, 0, j, 0)),
            pl.BlockSpec((1, seq, KV_LATENT), lambda b, j: (b, 0, 0)),
            pl.BlockSpec((1, seq, IDX_DIM), lambda b, j: (b, 0, 0)),
            pl.BlockSpec((IDX_HEADS * IDX_DIM, Q_TILE), lambda b, j: (0, b * nq + j)),
            pl.BlockSpec((IDX_HEADS, Q_TILE), lambda b, j: (0, b * nq + j)),
        ],
        out_specs=pl.BlockSpec((1, Q_TILE, N_HEADS * KV_LATENT), lambda b, j: (b, j, 0)),
        out_shape=jax.ShapeDtypeStruct((bsz, seq, N_HEADS * KV_LATENT), BF16),
        scratch_shapes=[
            pltpu.VMEM((seq, Q_TILE), F32),
            pltpu.VMEM((n_chunks, Q_TILE, K_CHUNK), F32),
            pltpu.VMEM((n_chunks, ATT_ROWS, K_CHUNK), F32),
            pltpu.VMEM((ATT_ROWS, KV_LATENT), F32),
            pltpu.VMEM((ATT_ROWS, 128), F32),
            pltpu.VMEM((ATT_ROWS, 128), F32),
        ],
        compiler_params=pltpu.CompilerParams(
            dimension_semantics=("arbitrary", "arbitrary"), vmem_limit_bytes=VMEM_LIMIT),
        name="dsa_attn",
    )(qlat, c, ki, qit, wit)


def _pool_kernel(h_ref, halo_ref, win_ref, wgrp_ref, scale_ref, y_ref):
    i = pl.program_id(1)
    tm = h_ref.shape[1]
    u_main = jnp.dot(h_ref[0].astype(BF16), win_ref[...], preferred_element_type=F32)
    u_halo = jnp.dot(halo_ref[0].astype(BF16), win_ref[...], preferred_element_type=F32)
    u_halo = u_halo * jnp.where(i > 0, 1.0, 0.0)
    u = jnp.concatenate([u_halo, u_main], axis=0)
    pos = i * tm + lax.broadcasted_iota(jnp.int32, (tm, 1), 0)
    for g, window in enumerate(POOL_WINDOWS):
        ug = u[:, g * POOL_GROUP_DIM:(g + 1) * POOL_GROUP_DIM]
        s = ug
        shift = 1
        while shift < window:
            s = s + pltpu.roll(s, shift, axis=0)
            shift *= 2
        cnt = jnp.minimum(pos + 1, window).astype(F32)
        pooled = (s[POOL_HALO:] / cnt - ug[POOL_HALO:]).astype(BF16)
        yg = jnp.dot(pooled, wgrp_ref[g], preferred_element_type=F32)
        yg = yg * scale_ref[:, g * POOL_GROUP_DIM:(g + 1) * POOL_GROUP_DIM]
        y_ref[0, :, g * POOL_GROUP_DIM:(g + 1) * POOL_GROUP_DIM] = yg.astype(BF16)


def _pool_call(h3, win, wgrp, scale, tm):
    bsz, seq, _ = h3.shape
    halo_per = tm // POOL_HALO
    return pl.pallas_call(
        _pool_kernel,
        grid=(bsz, seq // tm),
        in_specs=[
            pl.BlockSpec((1, tm, D_MODEL), lambda b, i: (b, i, 0)),
            pl.BlockSpec((1, POOL_HALO, D_MODEL), lambda b, i: (b, jnp.maximum(i * halo_per - 1, 0), 0)),
            pl.BlockSpec(win.shape, lambda b, i: (0, 0)),
            pl.BlockSpec(wgrp.shape, lambda b, i: (0, 0, 0)),
            pl.BlockSpec(scale.shape, lambda b, i: (0, 0)),
        ],
        out_specs=pl.BlockSpec((1, tm, D_MODEL), lambda b, i: (b, i, 0)),
        out_shape=jax.ShapeDtypeStruct((bsz, seq, D_MODEL), BF16),
        compiler_params=pltpu.CompilerParams(
            dimension_semantics=("arbitrary", "arbitrary"), vmem_limit_bytes=VMEM_LIMIT),
        name="pool_mix",
    )(h3, h3, win, wgrp, scale)


def _post_kernel(*refs, with_uv):
    if with_uv:
        (h_ref, mi_ref, wuv_ref, wmo_ref, g1_ref, b1_ref, wgu_ref, wd_ref, g2_ref, b2_ref,
         out_ref, h1_ref, hb_ref, acc_ref) = refs
    else:
        (h_ref, mi_ref, wmo_ref, g1_ref, b1_ref, wgu_ref, wd_ref, g2_ref, b2_ref,
         out_ref, h1_ref, hb_ref, acc_ref) = refs
    rows = POST_ROWS
    group = HEAD_GROUP * KV_LATENT

    for rs in [slice(r * rows, (r + 1) * rows) for r in range(h_ref.shape[0] // rows)]:
        if with_uv:
            mi = jnp.concatenate(
                [jnp.dot(mi_ref[rs, g * group:(g + 1) * group], wuv_ref[g],
                         preferred_element_type=F32).astype(BF16) for g in range(N_HEAD_GROUPS)], axis=1)
        else:
            mi = mi_ref[rs, :]
        mix = jnp.dot(mi, wmo_ref[...], preferred_element_type=F32)
        h1 = _layer_norm(DEEPNORM_ALPHA * h_ref[rs, :] + mix, g1_ref[...], b1_ref[...])
        h1_ref[rs, :] = h1
        hb_ref[rs, :] = h1.astype(BF16)
        acc_ref[rs, :] = jnp.zeros((rows, D_MODEL), F32)

        for f in range(N_FF_CHUNKS):
            hb = hb_ref[rs, :]
            cols = slice(f * FF_CHUNK, (f + 1) * FF_CHUNK)
            gate = jnp.dot(hb, wgu_ref[:, cols], preferred_element_type=F32)
            up = jnp.dot(hb, wgu_ref[:, D_FF + f * FF_CHUNK:D_FF + (f + 1) * FF_CHUNK],
                         preferred_element_type=F32)
            act = (gate * (1.0 / (1.0 + jnp.exp(-gate))) * up).astype(BF16)
            acc_ref[rs, :] += jnp.dot(act, wd_ref[cols, :], preferred_element_type=F32)
        out_ref[rs, :] = _layer_norm(DEEPNORM_ALPHA * h1_ref[rs, :] + acc_ref[rs, :],
                                     g2_ref[...], b2_ref[...])


def _post_call(h2, mi2, wuv, wmo, g1, b1, wgu, wd, g2, b2, tm):
    t = h2.shape[0]
    const2 = lambda i: (0, 0)
    const3 = lambda i: (0, 0, 0)
    resident = dict(pipeline_mode=pl.Buffered(1))
    with_uv = wuv is not None
    in_specs = [
        pl.BlockSpec((tm, D_MODEL), lambda i: (i, 0)),
        pl.BlockSpec((tm, mi2.shape[1]), lambda i: (i, 0)),
    ]
    operands = [h2, mi2]
    if with_uv:
        in_specs.append(pl.BlockSpec(wuv.shape, const3, **resident))
        operands.append(wuv)
    in_specs += [
        pl.BlockSpec(wmo.shape, const2, **resident),
        pl.BlockSpec(g1.shape, const2),
        pl.BlockSpec(b1.shape, const2),
        pl.BlockSpec(wgu.shape, const2, **resident),
        pl.BlockSpec(wd.shape, const2, **resident),
        pl.BlockSpec(g2.shape, const2),
        pl.BlockSpec(b2.shape, const2),
    ]
    operands += [wmo, g1, b1, wgu, wd, g2, b2]
    return pl.pallas_call(
        functools.partial(_post_kernel, with_uv=with_uv),
        grid=(t // tm,),
        in_specs=in_specs,
        out_specs=pl.BlockSpec((tm, D_MODEL), lambda i: (i, 0)),
        out_shape=jax.ShapeDtypeStruct((t, D_MODEL), F32),
        scratch_shapes=[
            pltpu.VMEM((tm, D_MODEL), F32),
            pltpu.VMEM((tm, D_MODEL), BF16),
            pltpu.VMEM((tm, D_MODEL), F32),
        ],
        compiler_params=pltpu.CompilerParams(
            dimension_semantics=("arbitrary",), vmem_limit_bytes=VMEM_LIMIT),
        name="post_ffn_uv" if with_uv else "post_ffn",
    )(*operands)


def kernel(x, a_w_in, a_w_uk, a_w_uv, a_kv_norm_g, a_w_o, b_w_in, b_w_grp, b_scale, b_w_o,
           f_w_gu, f_w_down, ln_mix_g, ln_mix_b, ln_ffn_g, ln_ffn_b):
    bsz, seq, _ = x.shape
    t = bsz * seq
    x2 = x.reshape(t, D_MODEL)
    row = lambda v: v.reshape(1, -1)

    w_in = a_w_in[0]
    o_c = D_MODEL + KV_LATENT
    o_qi = o_c + IDX_HEADS * IDX_DIM
    o_ki = o_qi + IDX_DIM
    w1 = w_in.astype(BF16)
    wwit = w_in[:, o_ki:].T.astype(BF16)
    eye = jnp.eye(HEAD_GROUP, dtype=F32)
    ukt = jnp.swapaxes(a_w_uk[0], 1, 2).reshape(N_HEAD_GROUPS, HEAD_GROUP, HEAD_DIM, KV_LATENT)
    wukbd = jnp.einsum('ghdc,hk->ghdkc', ukt, eye).reshape(
        N_HEAD_GROUPS, HEAD_GROUP * HEAD_DIM, HEAD_GROUP * KV_LATENT).astype(BF16)
    uv = a_w_uv[0].reshape(N_HEAD_GROUPS, HEAD_GROUP, KV_LATENT, HEAD_DIM)
    wuvbd = jnp.einsum('ghcd,hk->ghckd', uv, eye).reshape(
        N_HEAD_GROUPS, HEAD_GROUP * KV_LATENT, HEAD_GROUP * HEAD_DIM).astype(BF16)

    qlat, c, ki, qit, wit = _proj_call(x2, w1, wwit, wukbd, row(a_kv_norm_g[0]), bsz, seq, 512)
    olat = _attn_call(qlat, c, ki, qit, wit, bsz, seq)
    h = _post_call(x2, olat.reshape(t, N_HEADS * KV_LATENT), wuvbd, a_w_o[0].astype(BF16),
                   row(ln_mix_g[0]), row(ln_mix_b[0]), f_w_gu[0].astype(BF16), f_w_down[0].astype(BF16),
                   row(ln_ffn_g[0]), row(ln_ffn_b[0]), 512)

    y = _pool_call(h.reshape(bsz, seq, D_MODEL), b_w_in[0].astype(BF16), b_w_grp[0].astype(BF16),
                   row(b_scale[0]), 512)
    h = _post_call(h, y.reshape(t, D_MODEL), None, b_w_o[0].astype(BF16),
                   row(ln_mix_g[1]), row(ln_mix_b[1]), f_w_gu[1].astype(BF16), f_w_down[1].astype(BF16),
                   row(ln_ffn_g[1]), row(ln_ffn_b[1]), 512)
    return h.reshape(bsz, seq, D_MODEL)
```

```python
import functools
import math

import jax
import jax.numpy as jnp
from jax import lax
from jax.experimental import pallas as pl
from jax.experimental.pallas import tpu as pltpu

D_MODEL = 1024
N_HEADS = 16
HEAD_DIM = 64
KV_LATENT = 256
IDX_HEADS = 8
IDX_DIM = 64
TOPK = 256
POOL_WINDOWS = (2, 4, 8, 16)
POOL_GROUP_DIM = 256
D_FF = 2816
DEPTH = 2
DEEPNORM_ALPHA = (2 * DEPTH) ** 0.25
LN_EPS = 1e-5
RMS_EPS = 1e-6
LOG2E = math.log2(math.e)

Q_TILE = 256
K_CHUNK = 256
HEAD_GROUP = 4
N_HEAD_GROUPS = N_HEADS // HEAD_GROUP
ATT_HEADS = 8
ATT_ROWS = ATT_HEADS * Q_TILE
FF_CHUNK = 256
N_FF_CHUNKS = D_FF // FF_CHUNK
POOL_HALO = 16
MASK_BIAS = -1e30
N_PLAIN_SEARCH = 11
N_FIXED_SNAP = 4
MAX_SNAP_ITERS = 4096

BF16 = jnp.bfloat16
F32 = jnp.float32
LANES = 128
VMEM_LIMIT = 56 * 1024 * 1024


def _nt_dot(a, b):
    return lax.dot_general(a, b, (((1,), (1,)), ((), ())), preferred_element_type=F32)


def _layer_norm(z, g, b):
    mu = jnp.mean(z, axis=-1, keepdims=True)
    zc = z - mu
    var = jnp.mean(zc * zc, axis=-1, keepdims=True)
    return zc * lax.rsqrt(var + LN_EPS) * g + b


def _fold_rows(x, op, parts=4):
    x = x.reshape(parts, -1, 8, x.shape[-1])
    y = op(x, axis=1)
    return op(y, axis=0)


def _lane_rep(x, op):
    return jnp.broadcast_to(op(x, axis=0, keepdims=True), x.shape)


def _proj_kernel(x_ref, w1_ref, wukbd_ref, g_ref,
                 qlat_ref, c_ref, ki_ref, qit_ref, wit_ref):
    xb = x_ref[...].astype(BF16)
    o_c = D_MODEL + KV_LATENT
    o_qi = o_c + IDX_HEADS * IDX_DIM
    main = jnp.dot(xb, w1_ref[...], preferred_element_type=F32)
    ckv = main[:, D_MODEL:o_c]
    ms = jnp.mean(ckv * ckv, axis=-1, keepdims=True)
    c_ref[0] = (ckv * lax.rsqrt(ms + RMS_EPS) * g_ref[...]).astype(BF16)
    ki_ref[0] = main[:, o_qi:o_qi + IDX_DIM].astype(BF16)
    qscale = (HEAD_DIM ** -0.5) * LOG2E
    for g4 in range(N_HEAD_GROUPS):
        qg = main[:, g4 * 256:(g4 + 1) * 256].astype(BF16)
        ql = jnp.dot(qg, wukbd_ref[g4], preferred_element_type=F32) * qscale
        for hh in range(HEAD_GROUP):
            qlat_ref[0, g4 * HEAD_GROUP + hh] = ql[:, hh * KV_LATENT:(hh + 1) * KV_LATENT].astype(BF16)
    qit_ref[...] = (main[:, o_c:o_qi] * (IDX_DIM ** -0.5)).T.astype(BF16)
    wit_ref[...] = (main[:, o_qi + IDX_DIM:] * (IDX_HEADS ** -0.5)).T


def _proj_call(x2, w1, wukbd, g, bsz, seq, tm):
    t = bsz * seq
    per = seq // tm
    return pl.pallas_call(
        _proj_kernel,
        grid=(t // tm,),
        in_specs=[
            pl.BlockSpec((tm, D_MODEL), lambda i: (i, 0)),
            pl.BlockSpec(w1.shape, lambda i: (0, 0)),
            pl.BlockSpec(wukbd.shape, lambda i: (0, 0, 0)),
            pl.BlockSpec(g.shape, lambda i: (0, 0)),
        ],
        out_specs=[
            pl.BlockSpec((1, N_HEADS, tm, KV_LATENT), lambda i: (i // per, 0, i % per, 0)),
            pl.BlockSpec((1, tm, KV_LATENT), lambda i: (i // per, i % per, 0)),
            pl.BlockSpec((1, tm, IDX_DIM), lambda i: (i // per, i % per, 0)),
            pl.BlockSpec((IDX_HEADS * IDX_DIM, tm), lambda i: (0, i)),
            pl.BlockSpec((IDX_HEADS, tm), lambda i: (0, i)),
        ],
        out_shape=[
            jax.ShapeDtypeStruct((bsz, N_HEADS, seq, KV_LATENT), BF16),
            jax.ShapeDtypeStruct((bsz, seq, KV_LATENT), BF16),
            jax.ShapeDtypeStruct((bsz, seq, IDX_DIM), BF16),
            jax.ShapeDtypeStruct((IDX_HEADS * IDX_DIM, t), BF16),
            jax.ShapeDtypeStruct((IDX_HEADS, t), F32),
        ],
        compiler_params=pltpu.CompilerParams(
            dimension_semantics=("arbitrary",), vmem_limit_bytes=VMEM_LIMIT),
        name="dsa_proj",
    )(x2, w1, wukbd, g)


def _attn_kernel(qlat_ref, c_ref, ki_ref, qit_ref, wit_ref, o_ref,
                 st_ref, dm_ref, lg_ref, acc_ref, mx_ref, sm_ref):
    j = pl.program_id(1)
    nkc = j + 1
    q0 = j * Q_TILE
    neg_inf = jnp.float32(-jnp.inf)
    pos_inf = jnp.float32(jnp.inf)
    vshape = (8, Q_TILE)

    def chunk_loop(fn, init):
        def run(first, n, c):
            for t in range(n):
                c = fn(first + t, c)
            return c
        carry = lax.fori_loop(0, nkc // 4, lambda kq, c: run(4 * kq, 4, c), init)
        carry = lax.cond((nkc // 2) % 2 == 1, lambda c: run(4 * (nkc // 4), 2, c), lambda c: c, carry)
        return lax.cond(nkc % 2 == 1, lambda c: run(nkc - 1, 1, c), lambda c: c, carry)

    qi_all = jnp.concatenate(
        [qit_ref[h * IDX_DIM:(h + 1) * IDX_DIM, :] for h in range(IDX_HEADS)], axis=1)
    wit = wit_ref[...]
    qpos = q0 + lax.broadcasted_iota(jnp.int32, (K_CHUNK, Q_TILE), 1)
    krow = lax.broadcasted_iota(jnp.int32, (K_CHUNK, Q_TILE), 0)

    def score_body(kc, carry):
        mn, mxv = carry
        k0 = pl.multiple_of(kc * K_CHUNK, K_CHUNK)
        s_all = jnp.dot(ki_ref[0, pl.ds(k0, K_CHUNK), :], qi_all,
                        preferred_element_type=F32)
        s = wit[0:1, :] * jnp.maximum(s_all[:, 0:Q_TILE], 0.0)
        for h in range(1, IDX_HEADS):
            s = s + wit[h:h + 1, :] * jnp.maximum(s_all[:, h * Q_TILE:(h + 1) * Q_TILE], 0.0)
        valid = (k0 + krow) <= qpos
        masked = jnp.where(valid, s, neg_inf)
        st_ref[pl.ds(k0, K_CHUNK), :] = masked
        mn = jnp.minimum(mn, _fold_rows(jnp.where(valid, s, pos_inf), jnp.min))
        mxv = jnp.maximum(mxv, _fold_rows(masked, jnp.max))
        return mn, mxv

    mn, mxv = chunk_loop(score_body, (jnp.full(vshape, pos_inf, F32), jnp.full(vshape, neg_inf, F32)))
    rowmin = _lane_rep(mn, jnp.min)
    rowmax = _lane_rep(mxv, jnp.max)

    topk_f = jnp.float32(TOPK)

    def score_block(kc):
        k0 = pl.multiple_of(kc * K_CHUNK, K_CHUNK)
        return st_ref[pl.ds(k0, K_CHUNK), :]

    def count_ge(thr):
        def body(kc, acc):
            return acc + _fold_rows(jnp.where(score_block(kc) >= thr[0:1, :], 1.0, 0.0), jnp.sum)
        return _lane_rep(chunk_loop(body, jnp.zeros(vshape, F32)), jnp.sum)

    def count_snap(thr):
        def body(kc, carry):
            cnt, amin, bmx = carry
            blk = score_block(kc)
            ge = blk >= thr[0:1, :]
            cnt = cnt + _fold_rows(jnp.where(ge, 1.0, 0.0), jnp.sum)
            amin = jnp.minimum(amin, _fold_rows(jnp.where(ge, blk, pos_inf), jnp.min))
            bmx = jnp.maximum(bmx, _fold_rows(jnp.where(ge, neg_inf, blk), jnp.max))
            return cnt, amin, bmx
        cnt, amin, bmx = chunk_loop(
            body, (jnp.zeros(vshape, F32), jnp.full(vshape, pos_inf, F32), jnp.full(vshape, neg_inf, F32)))
        return _lane_rep(cnt, jnp.sum), _lane_rep(amin, jnp.min), _lane_rep(bmx, jnp.max)

    def unsettled(clo, bmin, bmax):
        return jnp.logical_and(clo > topk_f, bmin < bmax)

    def probe_point(lo, hi, bmin, bmax):
        a = jnp.maximum(lo, bmin)
        b = jnp.minimum(hi, bmax)
        mid = a * 0.5 + b * 0.5
        return jnp.where(mid <= lo, b, mid)

    def plain_body(i, state):
        lo, hi, clo, chi, bmin, bmax = state
        active = unsettled(clo, bmin, bmax)
        mid = probe_point(lo, hi, bmin, bmax)
        c = count_ge(mid)
        up = jnp.logical_and(active, c >= topk_f)
        dn = jnp.logical_and(active, c < topk_f)
        return (jnp.where(up, mid, lo), jnp.where(dn, mid, hi),
                jnp.where(up, c, clo), jnp.where(dn, c, chi), bmin, bmax)

    n_valid = (q0 + lax.broadcasted_iota(jnp.int32, vshape, 1) + 1).astype(F32)
    state = (rowmin, jnp.full(vshape, pos_inf, F32), n_valid, jnp.zeros(vshape, F32), rowmin, rowmax)
    searching = ((j + 1) * Q_TILE > TOPK).astype(jnp.int32)
    state = lax.fori_loop(0, N_PLAIN_SEARCH * searching, plain_body, state)

    def snap_flag(clo, bmin, bmax):
        return jnp.max(jnp.where(unsettled(clo, bmin, bmax), 1.0, 0.0))

    def snap_cond(carry):
        it, flag = carry[0], carry[1]
        return jnp.logical_and(flag > 0.0, it < MAX_SNAP_ITERS)

    def snap_step(i, state):
        lo, hi, clo, chi, bmin, bmax = state
        active = unsettled(clo, bmin, bmax)
        mid = probe_point(lo, hi, bmin, bmax)
        c, amin, bmx = count_snap(mid)
        up = jnp.logical_and(active, c >= topk_f)
        dn = jnp.logical_and(active, c < topk_f)
        return (jnp.where(up, mid, lo), jnp.where(dn, mid, hi), jnp.where(up, c, clo),
                jnp.where(dn, c, chi), jnp.where(up, amin, bmin), jnp.where(dn, bmx, bmax))

    def snap_body(carry):
        state = snap_step(0, carry[2:])
        return (carry[0] + 1, snap_flag(state[2], state[4], state[5])) + state

    state = lax.fori_loop(0, N_FIXED_SNAP * searching, snap_step, state)
    carry = lax.while_loop(snap_cond, snap_body,
                           (jnp.int32(0), snap_flag(state[2], state[4], state[5])) + state)
    _, _, lo, hi, clo, chi, bmin, _ = carry

    tie_r = clo[0:1, :] > topk_f
    thr_r = jnp.where(tie_r, hi[0:1, :], lo[0:1, :])
    tv_r = jnp.where(tie_r, bmin[0:1, :], pos_inf)
    need_r = topk_f - chi[0:1, :]
    tri = (lax.broadcasted_iota(jnp.int32, (K_CHUNK, K_CHUNK), 1)
           < lax.broadcasted_iota(jnp.int32, (K_CHUNK, K_CHUNK), 0)).astype(BF16)
    dist0 = (krow - qpos).astype(F32)

    def mask_chunk(kc, run):
        k0 = pl.multiple_of(kc * K_CHUNK, K_CHUNK)
        blk = st_ref[pl.ds(k0, K_CHUNK), :]
        dist = dist0 + (kc * K_CHUNK).astype(F32)
        eq = blk == tv_r
        eq_f = jnp.where(eq, 1.0, 0.0)
        rank = jnp.dot(tri, eq_f.astype(BF16), preferred_element_type=F32) + run
        by_score = jnp.where(blk >= thr_r, dist, MASK_BIAS)
        by_rank = jnp.where(rank < need_r, dist, MASK_BIAS)
        dm_ref[kc] = jnp.where(eq, by_rank, by_score).T
        return run + jnp.sum(eq_f, axis=0, keepdims=True)

    chunk_loop(mask_chunk, jnp.zeros((1, Q_TILE), F32))

    def fold_lanes(x, op):
        parts = [x[:, t * LANES:(t + 1) * LANES] for t in range(x.shape[1] // LANES)]
        while len(parts) > 1:
            parts = [op(parts[2 * t], parts[2 * t + 1]) for t in range(len(parts) // 2)]
        return parts[0]

    def block_loop(block):
        def quad(kq, _):
            block(4 * kq, 4)
            return 0

        lax.fori_loop(0, nkc // 4, quad, 0)

        @pl.when((nkc // 2) % 2 == 1)
        def _():
            block(4 * (nkc // 4), 2)

        @pl.when(nkc % 2 == 1)
        def _():
            block(nkc - 1, 1)

    for g in range(N_HEADS // ATT_HEADS):
        heads = range(g * ATT_HEADS, (g + 1) * ATT_HEADS)
        ql = qlat_ref[0, g * ATT_HEADS:(g + 1) * ATT_HEADS].reshape(ATT_ROWS, KV_LATENT)
        slopes = [LOG2E * 2.0 ** (-8.0 * (h + 1) / N_HEADS) for h in heads]
        mx_ref[...] = jnp.full(mx_ref.shape, neg_inf, F32)

        def logit_block(kc0, nch, ql=ql, slopes=slopes):
            k0 = pl.multiple_of(kc0 * K_CHUNK, K_CHUNK)
            lg = _nt_dot(ql, c_ref[0, pl.ds(k0, nch * K_CHUNK), :])
            dmk = jnp.concatenate([dm_ref[kc0 + t] for t in range(nch)], axis=1)
            for hh in range(ATT_HEADS):
                rs = slice(hh * Q_TILE, (hh + 1) * Q_TILE)
                lgh = lg[rs] + slopes[hh] * dmk
                for t in range(nch):
                    lg_ref[kc0 + t, rs, :] = lgh[:, t * K_CHUNK:(t + 1) * K_CHUNK]
                mx_ref[rs, :] = jnp.maximum(mx_ref[rs, :], fold_lanes(lgh, jnp.maximum))

        block_loop(logit_block)

        m = jnp.max(mx_ref[...], axis=-1, keepdims=True)

        acc_ref[...] = jnp.zeros(acc_ref.shape, F32)
        sm_ref[...] = jnp.zeros(sm_ref.shape, F32)

        def pv_block(kc0, nch, m=m):
            k0 = pl.multiple_of(kc0 * K_CHUNK, K_CHUNK)
            p = jnp.concatenate([jnp.exp2(lg_ref[kc0 + t] - m) for t in range(nch)], axis=1)
            sm_ref[...] += fold_lanes(p, jnp.add)
            acc_ref[...] += jnp.dot(p.astype(BF16), c_ref[0, pl.ds(k0, nch * K_CHUNK), :],
                                    preferred_element_type=F32)

        block_loop(pv_block)

        inv = 1.0 / jnp.sum(sm_ref[...], axis=-1, keepdims=True)
        for hh, h in enumerate(heads):
            rs = slice(hh * Q_TILE, (hh + 1) * Q_TILE)
            o_ref[0, :, h * KV_LATENT:(h + 1) * KV_LATENT] = (acc_ref[rs, :] * inv[rs]).astype(BF16)


def _attn_call(qlat, c, ki, qit, wit, bsz, seq):
    nq = seq // Q_TILE
    n_chunks = seq // K_CHUNK
    return pl.pallas_call(
        _attn_kernel,
        grid=(bsz, nq),
        in_specs=[
            pl.BlockSpec((1, N_HEADS, Q_TILE, KV_LATENT), lambda b, j: (b, 0, j, 0)),
            pl.BlockSpec((1, seq, KV_LATENT), lambda b, j: (b, 0, 0)),
            pl.BlockSpec((1, seq, IDX_DIM), lambda b, j: (b, 0, 0)),
            pl.BlockSpec((IDX_HEADS * IDX_DIM, Q_TILE), lambda b, j: (0, b * nq + j)),
            pl.BlockSpec((IDX_HEADS, Q_TILE), lambda b, j: (0, b * nq + j)),
        ],
        out_specs=pl.BlockSpec((1, Q_TILE, N_HEADS * KV_LATENT), lambda b, j: (b, j, 0)),
        out_shape=jax.ShapeDtypeStruct((bsz, seq, N_HEADS * KV_LATENT), BF16),
        scratch_shapes=[
            pltpu.VMEM((seq, Q_TILE), F32),
            pltpu.VMEM((n_chunks, Q_TILE, K_CHUNK), F32),
            pltpu.VMEM((n_chunks, ATT_ROWS, K_CHUNK), F32),
            pltpu.VMEM((ATT_ROWS, KV_LATENT), F32),
            pltpu.VMEM((ATT_ROWS, LANES), F32),
            pltpu.VMEM((ATT_ROWS, LANES), F32),
        ],
        compiler_params=pltpu.CompilerParams(
            dimension_semantics=("arbitrary", "arbitrary"), vmem_limit_bytes=VMEM_LIMIT),
        name="dsa_attn",
    )(qlat, c, ki, qit, wit)


def _pool_kernel(h_ref, halo_ref, win_ref, wgrp_ref, scale_ref, y_ref):
    i = pl.program_id(1)
    tm = h_ref.shape[1]
    u_main = jnp.dot(h_ref[0].astype(BF16), win_ref[...], preferred_element_type=F32)
    u_halo = jnp.dot(halo_ref[0].astype(BF16), win_ref[...], preferred_element_type=F32)
    u_halo = u_halo * jnp.where(i > 0, 1.0, 0.0)
    u = jnp.concatenate([u_halo, u_main], axis=0)
    pos = i * tm + lax.broadcasted_iota(jnp.int32, (tm, 1), 0)
    for g, window in enumerate(POOL_WINDOWS):
        ug = u[:, g * POOL_GROUP_DIM:(g + 1) * POOL_GROUP_DIM]
        s = ug
        shift = 1
        while shift < window:
            s = s + pltpu.roll(s, shift, axis=0)
            shift *= 2
        cnt = jnp.minimum(pos + 1, window).astype(F32)
        pooled = (s[POOL_HALO:] / cnt - ug[POOL_HALO:]).astype(BF16)
        yg = jnp.dot(pooled, wgrp_ref[g], preferred_element_type=F32)
        yg = yg * scale_ref[:, g * POOL_GROUP_DIM:(g + 1) * POOL_GROUP_DIM]
        y_ref[0, :, g * POOL_GROUP_DIM:(g + 1) * POOL_GROUP_DIM] = yg.astype(BF16)


def _pool_call(h3, win, wgrp, scale, tm):
    bsz, seq, _ = h3.shape
    halo_per = tm // POOL_HALO
    return pl.pallas_call(
        _pool_kernel,
        grid=(bsz, seq // tm),
        in_specs=[
            pl.BlockSpec((1, tm, D_MODEL), lambda b, i: (b, i, 0)),
            pl.BlockSpec((1, POOL_HALO, D_MODEL), lambda b, i: (b, jnp.maximum(i * halo_per - 1, 0), 0)),
            pl.BlockSpec(win.shape, lambda b, i: (0, 0)),
            pl.BlockSpec(wgrp.shape, lambda b, i: (0, 0, 0)),
            pl.BlockSpec(scale.shape, lambda b, i: (0, 0)),
        ],
        out_specs=pl.BlockSpec((1, tm, D_MODEL), lambda b, i: (b, i, 0)),
        out_shape=jax.ShapeDtypeStruct((bsz, seq, D_MODEL), BF16),
        compiler_params=pltpu.CompilerParams(
            dimension_semantics=("arbitrary", "arbitrary"), vmem_limit_bytes=VMEM_LIMIT),
        name="pool_mix",
    )(h3, h3, win, wgrp, scale)


def _post_kernel(*refs, with_uv):
    if with_uv:
        (h_ref, mi_ref, wuv_ref, wmo_ref, g1_ref, b1_ref, wgu_ref, wd_ref, g2_ref, b2_ref,
         out_ref, h1_ref, hb_ref, acc_ref) = refs
    else:
        (h_ref, mi_ref, wmo_ref, g1_ref, b1_ref, wgu_ref, wd_ref, g2_ref, b2_ref,
         out_ref, h1_ref, hb_ref, acc_ref) = refs
    group = HEAD_GROUP * KV_LATENT
    if with_uv:
        mi = jnp.concatenate(
            [jnp.dot(mi_ref[:, g * group:(g + 1) * group], wuv_ref[g],
                     preferred_element_type=F32).astype(BF16) for g in range(N_HEAD_GROUPS)], axis=1)
    else:
        mi = mi_ref[...]
    mix = jnp.dot(mi, wmo_ref[...], preferred_element_type=F32)
    h1 = _layer_norm(DEEPNORM_ALPHA * h_ref[...] + mix, g1_ref[...], b1_ref[...])
    h1_ref[...] = h1
    hb_ref[...] = h1.astype(BF16)
    acc_ref[...] = jnp.zeros(acc_ref.shape, F32)

    for f in range(N_FF_CHUNKS):
        hb = hb_ref[...]
        cols = slice(f * FF_CHUNK, (f + 1) * FF_CHUNK)
        gate = jnp.dot(hb, wgu_ref[0, :, cols], preferred_element_type=F32)
        up = jnp.dot(hb, wgu_ref[0, :, D_FF + f * FF_CHUNK:D_FF + (f + 1) * FF_CHUNK],
                     preferred_element_type=F32)
        act = (gate * (1.0 / (1.0 + jnp.exp(-gate))) * up).astype(BF16)
        acc_ref[...] += jnp.dot(act, wd_ref[0, cols, :], preferred_element_type=F32)
    out_ref[...] = _layer_norm(DEEPNORM_ALPHA * h1_ref[...] + acc_ref[...], g2_ref[...], b2_ref[...])


def _post_call(h2, mi2, wuv, wmo, g1, b1, wgu, wd, layer, g2, b2, tm):
    t = h2.shape[0]
    const2 = lambda i: (0, 0)
    const3 = lambda i: (0, 0, 0)
    resident = dict(pipeline_mode=pl.Buffered(1))
    with_uv = wuv is not None
    in_specs = [
        pl.BlockSpec((tm, D_MODEL), lambda i: (i, 0)),
        pl.BlockSpec((tm, mi2.shape[1]), lambda i: (i, 0)),
    ]
    operands = [h2, mi2]
    if with_uv:
        in_specs.append(pl.BlockSpec(wuv.shape, const3, **resident))
        operands.append(wuv)
    in_specs += [
        pl.BlockSpec(wmo.shape, const2, **resident),
        pl.BlockSpec(g1.shape, const2),
        pl.BlockSpec(b1.shape, const2),
        pl.BlockSpec((1,) + wgu.shape[1:], lambda i: (layer, 0, 0), **resident),
        pl.BlockSpec((1,) + wd.shape[1:], lambda i: (layer, 0, 0), **resident),
        pl.BlockSpec(g2.shape, const2),
        pl.BlockSpec(b2.shape, const2),
    ]
    operands += [wmo, g1, b1, wgu, wd, g2, b2]
    return pl.pallas_call(
        functools.partial(_post_kernel, with_uv=with_uv),
        grid=(t // tm,),
        in_specs=in_specs,
        out_specs=pl.BlockSpec((tm, D_MODEL), lambda i: (i, 0)),
        out_shape=jax.ShapeDtypeStruct((t, D_MODEL), F32),
        scratch_shapes=[
            pltpu.VMEM((tm, D_MODEL), F32),
            pltpu.VMEM((tm, D_MODEL), BF16),
            pltpu.VMEM((tm, D_MODEL), F32),
        ],
        compiler_params=pltpu.CompilerParams(
            dimension_semantics=("arbitrary",), vmem_limit_bytes=VMEM_LIMIT),
        name="post_ffn_uv" if with_uv else "post_ffn",
    )(*operands)


def kernel(x, a_w_in, a_w_uk, a_w_uv, a_kv_norm_g, a_w_o, b_w_in, b_w_grp, b_scale, b_w_o,
           f_w_gu, f_w_down, ln_mix_g, ln_mix_b, ln_ffn_g, ln_ffn_b):
    bsz, seq, _ = x.shape
    t = bsz * seq
    x2 = x.reshape(t, D_MODEL)
    row = lambda v: v.reshape(1, -1)

    w1 = a_w_in[0].astype(BF16)

    def block_diag(w, rows, cols):
        tiled = jnp.tile(w, (1, 1, HEAD_GROUP))
        r = lax.broadcasted_iota(jnp.int32, tiled.shape, 1) // rows
        c = lax.broadcasted_iota(jnp.int32, tiled.shape, 2) // cols
        return jnp.where(r == c, tiled, 0.0).astype(BF16)

    wukbd = block_diag(jnp.swapaxes(a_w_uk[0], 1, 2).reshape(N_HEAD_GROUPS, HEAD_GROUP * HEAD_DIM, KV_LATENT),
                       HEAD_DIM, KV_LATENT)
    wuvbd = block_diag(a_w_uv[0].reshape(N_HEAD_GROUPS, HEAD_GROUP * KV_LATENT, HEAD_DIM),
                       KV_LATENT, HEAD_DIM)

    wgu_all = f_w_gu.astype(BF16)
    wd_all = f_w_down.astype(BF16)

    qlat, c, ki, qit, wit = _proj_call(x2, w1, wukbd, row(a_kv_norm_g[0]), bsz, seq, 1024)
    olat = _attn_call(qlat, c, ki, qit, wit, bsz, seq)
    h = _post_call(x2, olat.reshape(t, N_HEADS * KV_LATENT), wuvbd, a_w_o[0].astype(BF16),
                   row(ln_mix_g[0]), row(ln_mix_b[0]), wgu_all, wd_all, 0,
                   row(ln_ffn_g[0]), row(ln_ffn_b[0]), 512)

    y = _pool_call(h.reshape(bsz, seq, D_MODEL), b_w_in[0].astype(BF16), b_w_grp[0].astype(BF16),
                   row(b_scale[0]), 1024)
    h = _post_call(h, y.reshape(t, D_MODEL), None, b_w_o[0].astype(BF16),
                   row(ln_mix_g[1]), row(ln_mix_b[1]), wgu_all, wd_all, 1,
                   row(ln_ffn_g[1]), row(ln_ffn_b[1]), 512)
    return h.reshape(bsz, seq, D_MODEL)
```
